```python
import math
import jax
import jax.numpy as jnp
from jax import lax
import numpy as np

D_MODEL = 1024
BATCH = 1
SEQ = 16384
DEPTH = 2
DEC_BATCH = 32
DEC_SEQ = 1
PAST_LEN = 16384
PAGE_SIZE = 128

HEAD_DIM = 64
N_RET = 8
N_SB = 8
N_FOX = 8
N_GDN = 8
RET_W = N_RET * HEAD_DIM
SB_W = N_SB * HEAD_DIM
FOX_W = N_FOX * HEAD_DIM
GDN_W = N_GDN * HEAD_DIM
MIX_W = RET_W + SB_W
MEM_LEN = 256
MEM_HEADS = 4
MEM_HEAD_DIM = 128
MEM_W = MEM_HEADS * MEM_HEAD_DIM
N_GROUPS = 4
EXPERTS_PER_GROUP = 4
N_EXPERTS = N_GROUPS * EXPERTS_PER_GROUP
TOP_K_IN_GROUP = 2
EXPERT_FF = 512
CONV_W = 4
Q_BLOCK = 128
RET_CHUNK = 128
GDN_CHUNK = 64
ROPE_BASE = 10000.0
FORGET_BIAS = 3.0
GATE_COL_SCALE = 0.1
EPS = 1e-6
N_EVEN = (DEPTH + 1) // 2
N_ODD = DEPTH // 2
EVEN_SPLITS = [RET_W, RET_W, RET_W, RET_W, SB_W, SB_W, SB_W]
ODD_SPLITS = [FOX_W, FOX_W, FOX_W, N_FOX, 3 * GDN_W, GDN_W, N_GDN, N_GDN]
EVEN_IN = sum(EVEN_SPLITS)
ODD_IN = sum(ODD_SPLITS)

kernel_name = 'hybrid_retention_stickbreak_fox_gdn_hmoe_step'

F32 = jnp.float32


def _split(x, sizes):
    return jnp.split(x, np.cumsum(sizes)[:-1].tolist(), axis=-1)


def _rmsnorm(x, g):
    xf = x.astype(F32)
    y = xf * lax.rsqrt(jnp.mean(xf * xf, axis=-1, keepdims=True) + EPS)
    return (y * g.astype(F32)).astype(x.dtype)


def _head_groupnorm(o, g):
    mu = jnp.mean(o, axis=-1, keepdims=True)
    var = jnp.mean(jnp.square(o - mu), axis=-1, keepdims=True)
    return (o - mu) * lax.rsqrt(var + EPS) * g.astype(F32).reshape(o.shape[2], o.shape[3])


def _l2norm(x):
    xf = x.astype(F32)
    return xf * lax.rsqrt(jnp.sum(xf * xf, axis=-1, keepdims=True) + EPS)


def _rotary(x, pos):
    half = x.shape[-1] // 2
    inv = ROPE_BASE ** (-jnp.arange(half, dtype=F32) / half)
    ang = pos.astype(F32)[:, None] * inv[None, :]
    cos = jnp.cos(ang)[None, :, None, :]
    sin = jnp.sin(ang)[None, :, None, :]
    xf = x.astype(F32)
    x1, x2 = xf[..., :half], xf[..., half:]
    return jnp.concatenate([x1 * cos - x2 * sin, x1 * sin + x2 * cos], axis=-1).astype(x.dtype)


def _blocks(a, blk):
    b, t = a.shape[0], a.shape[1]
    return jnp.moveaxis(a.reshape(b, t // blk, blk, *a.shape[2:]), 1, 0)


def _unblocks(a):
    a = jnp.moveaxis(a, 0, 1)
    return a.reshape(a.shape[0], -1, *a.shape[3:])


def _gather_pages(pool, layer, page_table):
    g = pool[layer, page_table]
    return g.reshape(g.shape[0], g.shape[1] * g.shape[2], *g.shape[3:])


def _retention(q, k, v, state):
    b, t, h, _ = q.shape
    c = math.gcd(t, RET_CHUNK)
    n = t // c
    lg = jnp.log(1.0 - jnp.exp2(-5.0 - jnp.arange(h, dtype=F32)))
    idx = jnp.arange(c, dtype=F32)
    diff = idx[:, None] - idx[None, :]
    causal = diff >= 0
    inner_decay = jnp.where(causal[None], jnp.exp(jnp.where(causal, diff, 0.0)[None] * lg[:, None, None]), 0.0)
    q_decay = jnp.exp((idx + 1.0)[None, :] * lg[:, None])
    k_decay = jnp.exp((c - 1.0 - idx)[None, :] * lg[:, None])
    chunk_decay = jnp.exp(c * lg)

    def to_chunks(a):
        return a.astype(F32).reshape(b, n, c, h, a.shape[-1]).transpose(1, 0, 3, 2, 4)

    def step(s, inp):
        qc, kc, vc = inp
        scores = jnp.einsum('bhqd,bhkd->bhqk', qc, kc) * inner_decay
        o = (jnp.einsum('bhqk,bhkv->bhqv', scores, vc)
             + jnp.einsum('bhqd,bhdv->bhqv', qc, s) * q_decay[None, :, :, None])
        s = s * chunk_decay[None, :, None, None] + jnp.einsum('bhkd,bhkv->bhdv', kc * k_decay[None, :, :, None], vc)
        return s, o

    s, o = lax.scan(step, state.astype(F32), (to_chunks(q), to_chunks(k), to_chunks(v)))
    return o.transpose(1, 0, 3, 2, 4).reshape(b, t, h, -1), s


def _stick_breaking(q, k, v, q_pos, k_pos):
    blk = math.gcd(q.shape[1], Q_BLOCK)
    scale = q.shape[-1] ** -0.5

    def one(args):
        qb, pb = args
        z = jnp.einsum('bqhd,bkhd->bhqk', qb, k).astype(F32) * scale
        strict = k_pos[None, :] < pb[:, None]
        log_keep = jnp.where(strict, jax.nn.log_sigmoid(-z), 0.0)
        after = lax.cumsum(log_keep, axis=3, reverse=True) - log_keep
        w = jnp.where(strict, jnp.exp(jax.nn.log_sigmoid(z) + after), 0.0)
        return jnp.einsum('bhqk,bkhd->bqhd', w.astype(v.dtype), v)

    return _unblocks(lax.map(one, (_blocks(q, blk), q_pos.reshape(-1, blk))))


def _forgetting_attn(q, k, v, cum_q, cum_k, q_pos, k_pos):
    blk = math.gcd(q.shape[1], Q_BLOCK)
    scale = q.shape[-1] ** -0.5
    ck = jnp.transpose(cum_k, (0, 2, 1))[:, :, None, :]

    def one(args):
        qb, cqb, pb = args
        s = jnp.einsum('bqhd,bkhd->bhqk', qb, k).astype(F32) * scale
        s = s + jnp.transpose(cqb, (0, 2, 1))[..., None] - ck
        causal = k_pos[None, :] <= pb[:, None]
        p = jax.nn.softmax(jnp.where(causal, s, -jnp.inf), axis=-1)
        return jnp.einsum('bhqk,bkhd->bqhd', p.astype(v.dtype), v)

    return _unblocks(lax.map(one, (_blocks(q, blk), _blocks(cum_q, blk), q_pos.reshape(-1, blk))))


def _causal_conv_silu(x, buf, w):
    t = x.shape[1]
    xp = jnp.concatenate([buf.astype(x.dtype), x], axis=1)
    y = sum(xp[:, i:i + t] * w[i] for i in range(CONV_W))
    return jax.nn.silu(y), xp[:, t:]


def _gated_delta(q, k, v, g, beta, state):
    b, t, h, _ = q.shape
    c = math.gcd(t, GDN_CHUNK)
    n = t // c

    def ch4(a):
        return a.astype(F32).reshape(b, n, c, h, a.shape[-1]).transpose(0, 3, 1, 2, 4)

    def ch3(a):
        return a.astype(F32).reshape(b, n, c, h).transpose(0, 3, 1, 2)

    qc, kc, vc = ch4(q), ch4(k), ch4(v)
    bc = ch3(beta)
    cg = jnp.cumsum(ch3(g), axis=-1)
    idx = jnp.arange(c)
    tril = idx[:, None] >= idx[None, :]
    strict = idx[:, None] > idx[None, :]
    dg = cg[..., :, None] - cg[..., None, :]
    decay = jnp.where(tril, jnp.exp(jnp.where(tril, dg, 0.0)), 0.0)
    k_beta = kc * bc[..., None]
    v_beta = vc * bc[..., None]
    lower = jnp.where(strict, jnp.einsum('bhnid,bhnjd->bhnij', k_beta, kc) * decay, 0.0)
    m = lower + jnp.eye(c, dtype=F32)
    u = lax.linalg.triangular_solve(m, v_beta, left_side=True, lower=True, unit_diagonal=True)
    w = lax.linalg.triangular_solve(m, k_beta * jnp.exp(cg)[..., None], left_side=True, lower=True, unit_diagonal=True)
    qk = jnp.einsum('bhnid,bhnjd->bhnij', qc, kc) * decay

    def step(s, inp):
        q_i, k_i, u_i, w_i, g_i, qk_i = inp
        v_new = u_i - jnp.einsum('bhcd,bhdv->bhcv', w_i, s)
        o = (jnp.einsum('bhcd,bhdv->bhcv', q_i * jnp.exp(g_i)[..., None], s)
             + jnp.einsum('bhij,bhjv->bhiv', qk_i, v_new))
        g_last = g_i[..., -1:]
        s = s * jnp.exp(g_last)[..., None] + jnp.einsum('bhcd,bhcv->bhdv', k_i * jnp.exp(g_last - g_i)[..., None], v_new)
        return s, o

    xs = tuple(jnp.moveaxis(a, 2, 0) for a in (qc, kc, u, w, cg, qk))
    s, o = lax.scan(step, state.astype(F32), xs)
    return o.transpose(1, 0, 3, 2, 4).reshape(b, t, h, -1), s


def _even_mixer(h, pos, ret_state, past_k, past_v, past_pos, w_in, g_ret, w_out):
    b, t, _ = h.shape
    rq, rk, rv, rg, sq, sk, sv = _split(h @ w_in, EVEN_SPLITS)

    def hd(a):
        return a.reshape(b, t, -1, HEAD_DIM)

    rq = _rotary(hd(rq), pos)
    rk = _rotary(hd(rk), pos) * HEAD_DIM ** -0.5
    o_ret, ret_state = _retention(rq, rk, hd(rv), ret_state)
    o_ret = _head_groupnorm(o_ret, g_ret).reshape(b, t, RET_W) * jax.nn.silu(rg.astype(F32))
    sq, sk, sv = hd(sq), hd(sk), hd(sv)
    if past_k is None:
        k_all, v_all, k_pos = sk, sv, pos
    else:
        k_all = jnp.concatenate([past_k.astype(sk.dtype), sk], axis=1)
        v_all = jnp.concatenate([past_v.astype(sv.dtype), sv], axis=1)
        k_pos = jnp.concatenate([past_pos, pos])
    o_sb = _stick_breaking(sq, k_all, v_all, pos, k_pos).reshape(b, t, SB_W)
    out = jnp.concatenate([o_ret.astype(h.dtype), o_sb.astype(h.dtype)], axis=-1) @ w_out
    return out, ret_state, sk, sv


def _odd_mixer(h, pos, gdn_state, conv_buf, past_k, past_v, past_logf, past_pos,
               w_in, b_forget, conv_w, a_log, dt_bias, g_gdn, w_out):
    b, t, _ = h.shape
    fq, fk, fv, f_logit, dqkv, dz, da, db = _split(h @ w_in, ODD_SPLITS)

    def hd(a):
        return a.reshape(b, t, -1, HEAD_DIM)

    fq, fk, fv = hd(fq), hd(fk), hd(fv)
    logf = jax.nn.log_sigmoid((f_logit + b_forget).astype(F32))
    if past_k is None:
        k_all, v_all, logf_all, k_pos = fk, fv, logf, pos
    else:
        k_all = jnp.concatenate([past_k.astype(fk.dtype), fk], axis=1)
        v_all = jnp.concatenate([past_v.astype(fv.dtype), fv], axis=1)
        logf_all = jnp.concatenate([past_logf.astype(F32), logf], axis=1)
        k_pos = jnp.concatenate([past_pos, pos])
    cum = jnp.cumsum(logf_all, axis=1)
    o_fox = _forgetting_attn(fq, k_all, v_all, cum[:, -t:], cum, pos, k_pos).reshape(b, t, FOX_W)

    conv_out, conv_buf = _causal_conv_silu(dqkv, conv_buf, conv_w)
    dq, dk, dv = _split(conv_out, [GDN_W, GDN_W, GDN_W])
    dq = _l2norm(hd(dq)) * HEAD_DIM ** -0.5
    dk = _l2norm(hd(dk))
    log_decay = -jnp.exp(a_log.astype(F32)) * jax.nn.softplus((da + dt_bias).astype(F32))
    beta = jax.nn.sigmoid(db.astype(F32))
    o_gdn, gdn_state = _gated_delta(dq, dk, hd(dv), log_decay, beta, gdn_state)
    o_gdn = (_rmsnorm(o_gdn, g_gdn) * jax.nn.silu(hd(dz).astype(F32))).reshape(b, t, GDN_W)
    out = jnp.concatenate([o_fox.astype(h.dtype), o_gdn.astype(h.dtype)], axis=-1) @ w_out
    return out, gdn_state, conv_buf, fk, fv, logf


def _mem_kv(mem, g_mem, w_mk, w_mv):
    b, m, _ = mem.shape
    hm = _rmsnorm(mem, g_mem)
    return ((hm @ w_mk).reshape(b, m, MEM_HEADS, MEM_HEAD_DIM),
            (hm @ w_mv).reshape(b, m, MEM_HEADS, MEM_HEAD_DIM))


def _mem_attn(h, mem_k, mem_v, w_q, w_o):
    b, t, _ = h.shape
    q = (h @ w_q).reshape(b, t, MEM_HEADS, MEM_HEAD_DIM)
    s = jnp.einsum('bqhd,bmhd->bhqm', q, mem_k.astype(q.dtype)).astype(F32) * MEM_HEAD_DIM ** -0.5
    p = jax.nn.softmax(s, axis=-1)
    o = jnp.einsum('bhqm,bmhd->bqhd', p.astype(q.dtype), mem_v.astype(q.dtype))
    return o.reshape(b, t, MEM_W) @ w_o


def _hier_moe(h, w_route_group, w_route_expert, w_exp_gate, w_exp_up, w_exp_down):
    b, t, d = h.shape
    x = h.reshape(b * t, d)
    g_prob = jax.nn.softmax((x @ w_route_group).astype(F32), axis=-1)
    g_val, g_idx = lax.top_k(g_prob, 1)
    e_logits = (x @ w_route_expert).astype(F32).reshape(-1, N_GROUPS, EXPERTS_PER_GROUP)
    e_sel = jnp.take_along_axis(e_logits, g_idx[:, :, None], axis=1)[:, 0]
    e_val, e_idx = lax.top_k(jax.nn.softmax(e_sel, axis=-1), TOP_K_IN_GROUP)
    e_val = e_val / jnp.sum(e_val, axis=-1, keepdims=True)
    expert_id = g_idx * EXPERTS_PER_GROUP + e_idx
    gate = jnp.sum(jax.nn.one_hot(expert_id, N_EXPERTS, dtype=F32) * (g_val * e_val)[..., None], axis=1)
    y = jnp.zeros((b * t, d), F32)
    for e in range(N_EXPERTS):
        he = jax.nn.silu(x @ w_exp_gate[e]) * (x @ w_exp_up[e])
        y = y + gate[:, e:e + 1] * (he @ w_exp_down[e]).astype(F32)
    return y.astype(h.dtype).reshape(b, t, d)


def setup_inputs(seed: int = 0) -> dict:
    key = jax.random.key(seed)
    keys = iter(jax.random.split(key, 64))

    def nrm(shape, scale=1.0):
        return jax.random.normal(next(keys), shape, F32) * scale

    def gain(shape):
        return 1.0 + nrm(shape, 0.1)

    n_pages = PAST_LEN // PAGE_SIZE
    used = DEC_BATCH * n_pages
    n_phys = used + max(1, used // 4)
    page_table = jax.random.permutation(next(keys), n_phys)[:used].reshape(DEC_BATCH, n_pages).astype(jnp.int32)

    odd_col_scale = jnp.concatenate([
        jnp.ones((3 * FOX_W,), F32), jnp.full((N_FOX,), GATE_COL_SCALE, F32),
        jnp.ones((4 * GDN_W,), F32), jnp.full((2 * N_GDN,), GATE_COL_SCALE, F32)])
    dt = jnp.exp(jax.random.uniform(next(keys), (N_ODD, N_GDN), F32, math.log(1e-3), math.log(1e-1)))
    return {
        'x_prompt': nrm((BATCH, SEQ, D_MODEL)),
        'x_sample': nrm((DEC_BATCH, DEC_SEQ, D_MODEL)),
        'state_ret': nrm((N_EVEN, DEC_BATCH, N_RET, HEAD_DIM, HEAD_DIM)),
        'cache_sb_k': nrm((N_EVEN, n_phys, PAGE_SIZE, N_SB, HEAD_DIM)),
        'cache_sb_v': nrm((N_EVEN, n_phys, PAGE_SIZE, N_SB, HEAD_DIM)),
        'cache_fox_k': nrm((N_ODD, n_phys, PAGE_SIZE, N_FOX, HEAD_DIM)),
        'cache_fox_v': nrm((N_ODD, n_phys, PAGE_SIZE, N_FOX, HEAD_DIM)),
        'cache_fox_logf': jax.nn.log_sigmoid(FORGET_BIAS + nrm((N_ODD, n_phys, PAGE_SIZE, N_FOX), 0.5)),
        'state_gdn': nrm((N_ODD, DEC_BATCH, N_GDN, HEAD_DIM, HEAD_DIM), 0.5),
        'state_conv': nrm((N_ODD, DEC_BATCH, CONV_W - 1, 3 * GDN_W)),
        'cache_mem_k': nrm((DEPTH, DEC_BATCH, MEM_LEN, MEM_HEADS, MEM_HEAD_DIM)),
        'cache_mem_v': nrm((DEPTH, DEC_BATCH, MEM_LEN, MEM_HEADS, MEM_HEAD_DIM)),
        'page_table': page_table,
        'mem_prompt': nrm((BATCH, MEM_LEN, D_MODEL)),
        'g_mix': gain((DEPTH, D_MODEL)),
        'ev_w_in': nrm((N_EVEN, D_MODEL, EVEN_IN), D_MODEL ** -0.5),
        'ev_g_ret': gain((N_EVEN, RET_W)),
        'ev_w_out': nrm((N_EVEN, MIX_W, D_MODEL), MIX_W ** -0.5),
        'od_w_in': nrm((N_ODD, D_MODEL, ODD_IN), D_MODEL ** -0.5) * odd_col_scale,
        'od_b_forget': FORGET_BIAS + nrm((N_ODD, N_FOX), 0.5),
        'od_conv_w': nrm((N_ODD, CONV_W, 3 * GDN_W), CONV_W ** -0.5),
        'od_a_log': jnp.log(jax.random.uniform(next(keys), (N_ODD, N_GDN), F32, 1.0, 16.0)),
        'od_dt_bias': dt + jnp.log(-jnp.expm1(-dt)),
        'od_g_gdn': gain((N_ODD, HEAD_DIM)),
        'od_w_out': nrm((N_ODD, MIX_W, D_MODEL), MIX_W ** -0.5),
        'g_xattn': gain((DEPTH, D_MODEL)),
        'g_mem': gain((DEPTH, D_MODEL)),
        'w_mq': nrm((DEPTH, D_MODEL, MEM_W), D_MODEL ** -0.5),
        'w_mk': nrm((DEPTH, D_MODEL, MEM_W), D_MODEL ** -0.5),
        'w_mv': nrm((DEPTH, D_MODEL, MEM_W), D_MODEL ** -0.5),
        'w_mo': nrm((DEPTH, MEM_W, D_MODEL), MEM_W ** -0.5),
        'g_ffn': gain((DEPTH, D_MODEL)),
        'w_route_group': nrm((DEPTH, D_MODEL, N_GROUPS), D_MODEL ** -0.5),
        'w_route_expert': nrm((DEPTH, D_MODEL, N_EXPERTS), D_MODEL ** -0.5),
        'w_exp_gate': nrm((DEPTH, N_EXPERTS, D_MODEL, EXPERT_FF), D_MODEL ** -0.5),
        'w_exp_up': nrm((DEPTH, N_EXPERTS, D_MODEL, EXPERT_FF), D_MODEL ** -0.5),
        'w_exp_down': nrm((DEPTH, N_EXPERTS, EXPERT_FF, D_MODEL), EXPERT_FF ** -0.5),
        'g_final': gain((D_MODEL,)),
    }


def reference(x_prompt, x_sample, state_ret, cache_sb_k, cache_sb_v, cache_fox_k, cache_fox_v,
              cache_fox_logf, state_gdn, state_conv, cache_mem_k, cache_mem_v, page_table, mem_prompt,
              g_mix, ev_w_in, ev_g_ret, ev_w_out, od_w_in, od_b_forget, od_conv_w, od_a_log, od_dt_bias,
              od_g_gdn, od_w_out, g_xattn, g_mem, w_mq, w_mk, w_mv, w_mo, g_ffn, w_route_group,
              w_route_expert, w_exp_gate, w_exp_up, w_exp_down, g_final):
    bp, tp, _ = x_prompt.shape
    bs, ts, _ = x_sample.shape
    past_len = page_table.shape[1] * PAGE_SIZE
    pos_p = jnp.arange(tp, dtype=jnp.int32)
    pos_s = past_len + jnp.arange(ts, dtype=jnp.int32)
    past_pos = jnp.arange(past_len, dtype=jnp.int32)

    yp, ys = x_prompt, x_sample
    ret_p, sbk_p, sbv_p, foxk_p, foxv_p, foxl_p, gdn_p, conv_p, memk_p, memv_p = [], [], [], [], [], [], [], [], [], []
    ret_s, sbk_s, sbv_s, foxk_s, foxv_s, foxl_s, gdn_s, conv_s = [], [], [], [], [], [], [], []
    for layer in range(DEPTH):
        i = layer // 2
        hp = _rmsnorm(yp, g_mix[layer])
        hs = _rmsnorm(ys, g_mix[layer])
        if layer % 2 == 0:
            mp, st, k, v = _even_mixer(hp, pos_p, jnp.zeros((bp, N_RET, HEAD_DIM, HEAD_DIM), F32), None, None, None,
                                       ev_w_in[i], ev_g_ret[i], ev_w_out[i])
            ret_p.append(st); sbk_p.append(k); sbv_p.append(v)
            ms, st, k, v = _even_mixer(hs, pos_s, state_ret[i],
                                       _gather_pages(cache_sb_k, i, page_table), _gather_pages(cache_sb_v, i, page_table),
                                       past_pos, ev_w_in[i], ev_g_ret[i], ev_w_out[i])
            ret_s.append(st); sbk_s.append(k); sbv_s.append(v)
        else:
            mp, st, buf, k, v, lf = _odd_mixer(
                hp, pos_p, jnp.zeros((bp, N_GDN, HEAD_DIM, HEAD_DIM), F32),
                jnp.zeros((bp, CONV_W - 1, 3 * GDN_W), hp.dtype), None, None, None, None,
                od_w_in[i], od_b_forget[i], od_conv_w[i], od_a_log[i], od_dt_bias[i], od_g_gdn[i], od_w_out[i])
            gdn_p.append(st); conv_p.append(buf); foxk_p.append(k); foxv_p.append(v); foxl_p.append(lf)
            ms, st, buf, k, v, lf = _odd_mixer(
                hs, pos_s, state_gdn[i], state_conv[i],
                _gather_pages(cache_fox_k, i, page_table), _gather_pages(cache_fox_v, i, page_table),
                _gather_pages(cache_fox_logf, i, page_table), past_pos,
                od_w_in[i], od_b_forget[i], od_conv_w[i], od_a_log[i], od_dt_bias[i], od_g_gdn[i], od_w_out[i])
            gdn_s.append(st); conv_s.append(buf); foxk_s.append(k); foxv_s.append(v); foxl_s.append(lf)
        yp = yp + mp
        ys = ys + ms
        mk, mv = _mem_kv(mem_prompt, g_mem[layer], w_mk[layer], w_mv[layer])
        memk_p.append(mk); memv_p.append(mv)
        yp = yp + _mem_attn(_rmsnorm(yp, g_xattn[layer]), mk, mv, w_mq[layer], w_mo[layer])
        ys = ys + _mem_attn(_rmsnorm(ys, g_xattn[layer]), cache_mem_k[layer], cache_mem_v[layer], w_mq[layer], w_mo[layer])
        yp = yp + _hier_moe(_rmsnorm(yp, g_ffn[layer]), w_route_group[layer], w_route_expert[layer],
                            w_exp_gate[layer], w_exp_up[layer], w_exp_down[layer])
        ys = ys + _hier_moe(_rmsnorm(ys, g_ffn[layer]), w_route_group[layer], w_route_expert[layer],
                            w_exp_gate[layer], w_exp_up[layer], w_exp_down[layer])

    y_prompt = _rmsnorm(yp, g_final)
    y_sample = _rmsnorm(ys, g_final)
    return (y_prompt, y_sample,
            jnp.stack(ret_p), jnp.stack(sbk_p), jnp.stack(sbv_p),
            jnp.stack(foxk_p), jnp.stack(foxv_p), jnp.stack(foxl_p), jnp.stack(gdn_p), jnp.stack(conv_p),
            jnp.stack(memk_p), jnp.stack(memv_p),
            jnp.stack(ret_s), jnp.stack(sbk_s), jnp.stack(sbv_s),
            jnp.stack(foxk_s), jnp.stack(foxv_s), jnp.stack(foxl_s), jnp.stack(gdn_s), jnp.stack(conv_s))
```

```python
import functools
import math

import jax
import jax.numpy as jnp
from jax import lax
from jax.experimental import pallas as pl
from jax.experimental.pallas import tpu as pltpu

F32 = jnp.float32
BF16 = jnp.bfloat16
I32 = jnp.int32

HEAD_DIM = 64
N_HEADS = 8
MIX_HALF = N_HEADS * HEAD_DIM
MEM_HEADS = 4
MEM_HEAD_DIM = 128
N_GROUPS = 4
EXPERTS_PER_GROUP = 4
N_EXPERTS = N_GROUPS * EXPERTS_PER_GROUP
CONV_W = 4
RET_CHUNK = 128
GDN_CHUNK = 64
ROPE_BASE = 10000.0
EPS = 1e-6
LOG2E = 1.4426950408889634
LN2 = 0.6931471805599453

LANES = 128
SUBLANES = 8
VMEM_LIMIT_BYTES = 56 * 1024 * 1024

HIGHEST = lax.Precision.HIGHEST


def _cparams(*sem):
    return pltpu.CompilerParams(dimension_semantics=sem, vmem_limit_bytes=VMEM_LIMIT_BYTES)


def _dot(a, b):
    return jnp.dot(a, b, preferred_element_type=F32)


def _dot_nt(a, b):
    return lax.dot_general(a, b, (((1,), (1,)), ((), ())), preferred_element_type=F32)


def _dot_hi(a, b):
    return jnp.dot(a, b, preferred_element_type=F32, precision=HIGHEST)


def _dot_nt_hi(a, b):
    return lax.dot_general(a, b, (((1,), (1,)), ((), ())), preferred_element_type=F32, precision=HIGHEST)


def _split2(x):
    hi = x.astype(BF16)
    lo = (x - hi.astype(F32)).astype(BF16)
    return hi, lo


def _split3(x):
    hi = x.astype(BF16)
    r = x - hi.astype(F32)
    mid = r.astype(BF16)
    lo = (r - mid.astype(F32)).astype(BF16)
    return hi, mid, lo


def _dot_x3(x, m_bf16):
    hi, mid, lo = _split3(x)
    return _dot(hi, m_bf16) + _dot(mid, m_bf16) + _dot(lo, m_bf16)


def _dot_3x(m_bf16, x):
    hi, mid, lo = _split3(x)
    return _dot(m_bf16, hi) + _dot(m_bf16, mid) + _dot(m_bf16, lo)


def _rms(x, g):
    return x * lax.rsqrt(jnp.mean(x * x, axis=-1, keepdims=True) + EPS) * g


def _silu(x):
    return x / (1.0 + jnp.exp(-x))


def _softplus(x):
    return jnp.maximum(x, 0.0) + jnp.log(1.0 + jnp.exp(-jnp.abs(x)))


def _iota(shape, dim):
    return lax.broadcasted_iota(I32, shape, dim)


def _head_block_ones(n, dtype=BF16):
    r = _iota((n, n), 0) // HEAD_DIM
    c = _iota((n, n), 1) // HEAD_DIM
    return jnp.where(r == c, 1.0, 0.0).astype(dtype)


def _norm_proj(x, g, w, outs, *, wt=None, precise=False, tm=512):
    m, d = x.shape
    tm = min(tm, m)
    assert m % tm == 0
    n_out = sum(len(o[2]) for o in outs)

    def body(*refs):
        x_ref, g_ref, w_ref = refs[:3]
        pos = 3
        wt_ref = None
        if wt is not None:
            wt_ref = refs[3]
            pos = 4
        o_refs = refs[pos:]
        xn = _rms(x_ref[...], g_ref[...])
        xb = xn if precise else xn.astype(BF16)
        k = 0
        for start, width, variants in outs:
            wslab = w_ref[:, start:start + width]
            r = _dot_hi(xb, wslab) if precise else _dot(xb, wslab)
            for dtype, scale in variants:
                o_refs[k][...] = (r if scale == 1.0 else r * scale).astype(dtype)
                k += 1
        if wt_ref is not None:
            o_refs[k][...] = _dot_nt_hi(wt_ref[...], xb) if precise else _dot_nt(wt_ref[...], xb)

    in_specs = [pl.BlockSpec((tm, d), lambda i: (i, 0)),
                pl.BlockSpec((1, d), lambda i: (0, 0)),
                pl.BlockSpec(w.shape, lambda i: (0, 0))]
    args = [x, g, w]
    if wt is not None:
        in_specs.append(pl.BlockSpec(wt.shape, lambda i: (0, 0)))
        args.append(wt)
    out_shape, out_specs = [], []
    for start, width, variants in outs:
        for dtype, _ in variants:
            out_shape.append(jax.ShapeDtypeStruct((m, width), dtype))
            out_specs.append(pl.BlockSpec((tm, width), lambda i: (i, 0)))
    if wt is not None:
        out_shape.append(jax.ShapeDtypeStruct((wt.shape[0], m), F32))
        out_specs.append(pl.BlockSpec((wt.shape[0], tm), lambda i: (0, i)))
    res = pl.pallas_call(
        body, grid=(m // tm,), in_specs=in_specs, out_specs=out_specs, out_shape=out_shape,
        compiler_params=_cparams("parallel"), name="norm_proj")(*args)
    assert len(res) == n_out + (wt is not None)
    return res


def _resid_proj(res, a_list, w_list, *, precise=False, tm=512, final_g=None):
    m, d = res.shape
    tm = min(tm, m)
    n = len(a_list)

    def body(*refs):
        r_ref = refs[0]
        a_refs = refs[1:1 + n]
        w_refs = refs[1 + n:1 + 2 * n]
        pos = 1 + 2 * n
        acc = r_ref[...]
        for a_ref, w_ref in zip(a_refs, w_refs):
            acc = acc + (_dot_hi(a_ref[...], w_ref[...]) if precise else _dot(a_ref[...], w_ref[...]))
        if final_g is None:
            refs[pos][...] = acc
        else:
            refs[pos + 1][...] = acc
            refs[pos + 2][...] = _rms(acc, refs[pos][...])

    in_specs = [pl.BlockSpec((tm, d), lambda i: (i, 0))]
    in_specs += [pl.BlockSpec((tm, a.shape[1]), lambda i: (i, 0)) for a in a_list]
    in_specs += [pl.BlockSpec(w.shape, lambda i: (0, 0)) for w in w_list]
    args = [res, *a_list, *w_list]
    out_shape = [jax.ShapeDtypeStruct((m, d), F32)]
    out_specs = [pl.BlockSpec((tm, d), lambda i: (i, 0))]
    if final_g is not None:
        in_specs.append(pl.BlockSpec((1, d), lambda i: (0, 0)))
        args.append(final_g)
        out_shape.append(jax.ShapeDtypeStruct((m, d), F32))
        out_specs.append(pl.BlockSpec((tm, d), lambda i: (i, 0)))
    out = pl.pallas_call(
        body, grid=(m // tm,), in_specs=in_specs, out_specs=out_specs, out_shape=out_shape,
        compiler_params=_cparams("parallel"), name="resid_proj")(*args)
    return out[0] if final_g is None else out


def _rope_tables(pos, inv_lane):
    ang = pos * inv_lane
    return jnp.cos(ang), jnp.sin(ang)


def _rotary_lanes(x, cos, sin, first_half):
    w = x.shape[-1]
    reps = w // LANES
    c = jnp.concatenate([cos] * reps, axis=-1) if reps > 1 else cos
    s = jnp.concatenate([sin] * reps, axis=-1) if reps > 1 else sin
    x_up = pltpu.roll(x, w - HEAD_DIM // 2, 1)
    x_dn = pltpu.roll(x, HEAD_DIM // 2, 1)
    return x * c + jnp.where(first_half, -x_up, x_dn) * s


def _retention_prompt(rq, rk, rv, rg, g_ret, inv_lane):
    t, w = rq.shape
    c = math.gcd(t, RET_CHUNK)
    n = t // c
    scale = HEAD_DIM ** -0.5

    def body(q_ref, k_ref, v_ref, gate_ref, gret_ref, inv_ref, o_ref, s_out_ref,
             s_scr, dec_scr, qd_scr, kd_scr):
        i = pl.program_id(0)
        lane = _iota((c, w), 1)
        head_of_lane = lane // HEAD_DIM
        first_half = (lane % HEAD_DIM) < (HEAD_DIM // 2)
        lg_lane = jnp.log(1.0 - jnp.exp2(-5.0 - head_of_lane.astype(F32)))
        row = _iota((c, w), 0).astype(F32)

        @pl.when(i == 0)
        def _():
            s_scr[...] = jnp.zeros_like(s_scr)
            qd_scr[...] = jnp.exp((row + 1.0) * lg_lane)
            kd_scr[...] = jnp.exp((c - 1.0 - row) * lg_lane)
            ri = _iota((c, c), 0)
            ci = _iota((c, c), 1)
            diff = (ri - ci).astype(F32)
            for h in range(N_HEADS):
                lg_h = math.log(1.0 - 2.0 ** (-5.0 - h))
                dec_scr[h] = jnp.where(ri >= ci, jnp.exp(jnp.where(ri >= ci, diff, 0.0) * lg_h), 0.0)

        pos = (i * c).astype(F32) + _iota((c, 1), 0).astype(F32)
        cos, sin = _rope_tables(pos, inv_ref[...])
        q = _rotary_lanes(q_ref[...], cos, sin, first_half)
        k = _rotary_lanes(k_ref[...], cos, sin, first_half) * scale
        qb = q.astype(BF16)
        kb = k.astype(BF16)
        vb = v_ref[...].astype(BF16)
        s_old = s_scr[...]
        o = _dot(qb, s_old.astype(BF16)) * qd_scr[...]
        for h in range(N_HEADS):
            hm = head_of_lane == h
            qh = jnp.where(hm, qb, jnp.zeros_like(qb))
            sc = _dot_nt(qh, kb) * dec_scr[h]
            oh = _dot(sc.astype(BF16), vb)
            o = o + jnp.where(hm, oh, 0.0)
        kd = (k * kd_scr[...]).astype(BF16)
        upd = lax.dot_general(kd, vb, (((0,), (0,)), ((), ())), preferred_element_type=F32)
        blk = (_iota((w, w), 0) // HEAD_DIM) == (_iota((w, w), 1) // HEAD_DIM)
        cd = jnp.exp(float(c) * jnp.log(1.0 - jnp.exp2(-5.0 - (_iota((1, w), 1) // HEAD_DIM).astype(F32))))
        s_scr[...] = s_old * cd + jnp.where(blk, upd, 0.0)
        ones = _head_block_ones(w)
        inv_n = 1.0 / HEAD_DIM
        hi, lo = _split2(o)
        mu = (_dot(hi, ones) + _dot(lo, ones)) * inv_n
        dlt = o - mu
        hi, lo = _split2(dlt * dlt)
        var = (_dot(hi, ones) + _dot(lo, ones)) * inv_n
        y = dlt * lax.rsqrt(var + EPS) * gret_ref[...]
        o_ref[...] = (y * _silu(gate_ref[...])).astype(o_ref.dtype)

        @pl.when(i == n - 1)
        def _():
            s_fin = s_scr[...]
            for h in range(N_HEADS):
                s_out_ref[h] = s_fin[h * HEAD_DIM:(h + 1) * HEAD_DIM, h * HEAD_DIM:(h + 1) * HEAD_DIM]

    tok = pl.BlockSpec((c, w), lambda i: (i, 0))
    o, s = pl.pallas_call(
        body, grid=(n,),
        in_specs=[tok, tok, tok, tok, pl.BlockSpec((1, w), lambda i: (0, 0)),
                  pl.BlockSpec((1, LANES), lambda i: (0, 0))],
        out_specs=[tok, pl.BlockSpec((N_HEADS, HEAD_DIM, HEAD_DIM), lambda i: (0, 0, 0))],
        out_shape=[jax.ShapeDtypeStruct((t, w), BF16),
                   jax.ShapeDtypeStruct((N_HEADS, HEAD_DIM, HEAD_DIM), F32)],
        scratch_shapes=[pltpu.VMEM((w, w), F32), pltpu.VMEM((N_HEADS, c, c), F32),
                        pltpu.VMEM((c, w), F32), pltpu.VMEM((c, w), F32)],
        compiler_params=_cparams("arbitrary"), name="retention_prompt")(rq, rk, rv, rg, g_ret, inv_lane)
    return o, s


def _attn_block(t):
    for b in (256, 128):
        if t % b == 0:
            return b
    raise ValueError(f"sequence length {t} must be a multiple of 128")


def _strict_upper(n):
    return jnp.where(_iota((n, n), 0) > _iota((n, n), 1), 1.0, 0.0).astype(BF16)


def _sb_prompt(qb, kb, vb):
    t, w = qb.shape
    blk = _attn_block(t)
    nq = t // blk
    n_pairs = w // LANES

    def body(q_ref, k_ref, v_ref, o_ref, acc_scr, c_scr):
        qi = pl.program_id(1)
        q = q_ref[...]
        lane = _iota((blk, LANES), 1)
        zero = jnp.zeros_like(q)
        q_heads = (jnp.where(lane < HEAD_DIM, q, zero), jnp.where(lane >= HEAD_DIM, q, zero))
        tri = _strict_upper(blk)
        acc_scr[...] = jnp.zeros_like(acc_scr)
        c_scr[...] = jnp.zeros_like(c_scr)

        def block(kj, strict):
            start = pl.multiple_of(kj * blk, blk)
            k = k_ref[pl.ds(start, blk), :]
            v = v_ref[pl.ds(start, blk), :]
            for hh in range(2):
                z = _dot_nt(q_heads[hh], k)
                sp = jnp.maximum(z, 0.0) + jnp.log(1.0 + jnp.exp2(-jnp.abs(z))) * LOG2E
                if strict is not None:
                    sp = jnp.where(strict, sp, 0.0)
                hi, lo = _split2(sp)
                after = _dot(hi, tri) + _dot(lo, tri)
                carry = c_scr[hh]
                wgt = jnp.exp2(z - sp - after - carry)
                if strict is not None:
                    wgt = jnp.where(strict, wgt, 0.0)
                acc_scr[hh] += _dot(wgt.astype(BF16), v)
                c_scr[hh] = carry + jnp.sum(sp, axis=1, keepdims=True)

        block(qi, _iota((blk, blk), 1) < _iota((blk, blk), 0))

        def step(j, _):
            block(qi - 1 - j, None)
            return 0

        lax.fori_loop(0, qi, step, 0)
        o_ref[...] = jnp.where(lane < HEAD_DIM, acc_scr[0], acc_scr[1]).astype(o_ref.dtype)

    return pl.pallas_call(
        body, grid=(n_pairs, nq),
        in_specs=[pl.BlockSpec((blk, LANES), lambda p, i: (i, p)),
                  pl.BlockSpec((t, LANES), lambda p, i: (0, p)),
                  pl.BlockSpec((t, LANES), lambda p, i: (0, p))],
        out_specs=pl.BlockSpec((blk, LANES), lambda p, i: (i, p)),
        out_shape=jax.ShapeDtypeStruct((t, w), BF16),
        scratch_shapes=[pltpu.VMEM((2, blk, LANES), F32), pltpu.VMEM((2, blk, 1), F32)],
        compiler_params=_cparams("parallel", "arbitrary"), name="sb_prompt")(qb, kb, vb)


def _fox_prompt(qb, kb, vb, cum_col, cum_row):
    t, w = qb.shape
    blk = _attn_block(t)
    nq = t // blk
    n_pairs = w // LANES

    def body(q_ref, k_ref, v_ref, cq_ref, ck_ref, o_ref, acc_scr, m_scr, l_scr):
        p = pl.program_id(0)
        qi = pl.program_id(1)
        q = q_ref[...]
        lane = _iota((blk, LANES), 1)
        zero = jnp.zeros_like(q)
        q_heads = (jnp.where(lane < HEAD_DIM, q, zero), jnp.where(lane >= HEAD_DIM, q, zero))
        cq_all = cq_ref[...]
        hsel = _iota((blk, N_HEADS), 1)
        cq = [jnp.sum(jnp.where(hsel == 2 * p + hh, cq_all, 0.0), axis=1, keepdims=True) for hh in range(2)]
        acc_scr[...] = jnp.zeros_like(acc_scr)
        m_scr[...] = jnp.full_like(m_scr, -jnp.inf)
        l_scr[...] = jnp.zeros_like(l_scr)
        rsel = _iota((N_HEADS, blk), 0)

        def block(kj, causal):
            start = pl.multiple_of(kj * blk, blk)
            k = k_ref[pl.ds(start, blk), :]
            v = v_ref[pl.ds(start, blk), :]
            ck_all = ck_ref[:, pl.ds(start, blk)]
            for hh in range(2):
                ck = jnp.sum(jnp.where(rsel == 2 * p + hh, ck_all, 0.0), axis=0, keepdims=True)
                s = _dot_nt(q_heads[hh], k) + (cq[hh] - ck)
                if causal is not None:
                    s = jnp.where(causal, s, -jnp.inf)
                m_old = m_scr[hh]
                m_new = jnp.maximum(m_old, jnp.max(s, axis=1, keepdims=True))
                alpha = jnp.exp2(m_old - m_new)
                pr = jnp.exp2(s - m_new)
                l_scr[hh] = alpha * l_scr[hh] + jnp.sum(pr, axis=1, keepdims=True)
                acc_scr[hh] = alpha * acc_scr[hh] + _dot(pr.astype(BF16), v)
                m_scr[hh] = m_new

        block(qi, _iota((blk, blk), 1) <= _iota((blk, blk), 0))

        def step(j, _):
            block(qi - 1 - j, None)
            return 0

        lax.fori_loop(0, qi, step, 0)
        out = jnp.where(lane < HEAD_DIM, acc_scr[0] / l_scr[0], acc_scr[1] / l_scr[1])
        o_ref[...] = out.astype(o_ref.dtype)

    return pl.pallas_call(
        body, grid=(n_pairs, nq),
        in_specs=[pl.BlockSpec((blk, LANES), lambda p, i: (i, p)),
                  pl.BlockSpec((t, LANES), lambda p, i: (0, p)),
                  pl.BlockSpec((t, LANES), lambda p, i: (0, p)),
                  pl.BlockSpec((blk, N_HEADS), lambda p, i: (i, 0)),
                  pl.BlockSpec((N_HEADS, t), lambda p, i: (0, 0))],
        out_specs=pl.BlockSpec((blk, LANES), lambda p, i: (i, p)),
        out_shape=jax.ShapeDtypeStruct((t, w), BF16),
        scratch_shapes=[pltpu.VMEM((2, blk, LANES), F32), pltpu.VMEM((2, blk, 1), F32),
                        pltpu.VMEM((2, blk, 1), F32)],
        compiler_params=_cparams("parallel", "arbitrary"), name="fox_prompt")(qb, kb, vb, cum_col, cum_row)


def _gate_math(x, bias, a_log, idx):
    xb = x + bias
    e = jnp.exp(-jnp.abs(xb))
    l1p = jnp.log(1.0 + e)
    logf = jnp.minimum(xb, 0.0) - l1p
    g = -jnp.exp(a_log) * (jnp.maximum(xb, 0.0) + l1p)
    beta = 1.0 / (1.0 + jnp.exp(-xb))
    h = N_HEADS
    return jnp.where(idx < h, logf, jnp.where(idx < 2 * h, g, jnp.where(idx < 3 * h, beta, 0.0)))


def _gates(small_col, small_row, bias_lane, alog_lane, bias_col, alog_col, *, tm=2048):
    m = small_col.shape[0]
    r = small_row.shape[0]
    tm = min(tm, m)

    def body(c_ref, r_ref, bl_ref, al_ref, bc_ref, ac_ref, oc_ref, or_ref):
        oc_ref[...] = _gate_math(c_ref[...], bl_ref[...], al_ref[...], _iota((tm, LANES), 1))
        or_ref[...] = _gate_math(r_ref[...], bc_ref[...], ac_ref[...], _iota((r, tm), 0))

    return pl.pallas_call(
        body, grid=(m // tm,),
        in_specs=[pl.BlockSpec((tm, LANES), lambda i: (i, 0)), pl.BlockSpec((r, tm), lambda i: (0, i)),
                  pl.BlockSpec((1, LANES), lambda i: (0, 0)), pl.BlockSpec((1, LANES), lambda i: (0, 0)),
                  pl.BlockSpec((r, 1), lambda i: (0, 0)), pl.BlockSpec((r, 1), lambda i: (0, 0))],
        out_specs=[pl.BlockSpec((tm, LANES), lambda i: (i, 0)), pl.BlockSpec((r, tm), lambda i: (0, i))],
        out_shape=[jax.ShapeDtypeStruct((m, LANES), F32), jax.ShapeDtypeStruct((r, m), F32)],
        compiler_params=_cparams("parallel"), name="gates")(
            small_col, small_row, bias_lane, alog_lane, bias_col, alog_col)


def _cum_forget(logf_rows, n_chunks):
    r = logf_rows.shape[0]

    def body(x_ref, o_ref):
        x = x_ref[...]
        incl = jnp.where(_iota((LANES, LANES), 0) <= _iota((LANES, LANES), 1), 1.0, 0.0).astype(BF16)
        local = _dot_x3(x, incl)
        tot = jnp.broadcast_to(local[:, LANES - 1:LANES], (r, LANES))
        ri = _iota((r, r), 0)
        ci = _iota((r, r), 1)
        before = jnp.where((ri // n_chunks == ci // n_chunks) & (ci < ri), 1.0, 0.0).astype(BF16)
        o_ref[...] = (local + _dot_3x(before, tot)) * LOG2E

    return pl.pallas_call(
        body, grid=(1,),
        in_specs=[pl.BlockSpec((r, LANES), lambda i: (0, 0))],
        out_specs=pl.BlockSpec((r, LANES), lambda i: (0, 0)),
        out_shape=jax.ShapeDtypeStruct((r, LANES), F32),
        compiler_params=_cparams("arbitrary"), name="cum_forget")(logf_rows)


def _gdn_prep(dqkv, conv_w, *, tm=256):
    t, w3 = dqkv.shape
    w = w3 // 3
    tm = min(tm, t)
    scale = HEAD_DIM ** -0.5

    def body(x_ref, prev_ref, cw_ref, q_ref, k_ref, kt_ref, v_ref):
        i = pl.program_id(0)
        x = x_ref[...]
        prev = jnp.where(i == 0, 0.0, prev_ref[...])
        xp = jnp.concatenate([prev, x], axis=0)
        cw = cw_ref[...]
        off = SUBLANES - (CONV_W - 1)
        y = xp[off:off + tm] * cw[0:1]
        for j in range(1, CONV_W):
            y = y + xp[off + j:off + j + tm] * cw[j:j + 1]
        y = _silu(y)
        ones = _head_block_ones(w)

        def l2n(a):
            hi, lo = _split2(a * a)
            return a * lax.rsqrt(_dot(hi, ones) + _dot(lo, ones) + EPS)

        qn = l2n(y[:, :w]) * scale
        kn = l2n(y[:, w:2 * w])
        vv = y[:, 2 * w:]
        for h in range(N_HEADS):
            sl = slice(h * HEAD_DIM, (h + 1) * HEAD_DIM)
            q_ref[h] = qn[:, sl]
            k_ref[h] = kn[:, sl]
            v_ref[h] = vv[:, sl]
        kt_ref[...] = kn.T

    hm = pl.BlockSpec((N_HEADS, tm, HEAD_DIM), lambda i: (0, i, 0))
    hm_shape = jax.ShapeDtypeStruct((N_HEADS, t, HEAD_DIM), F32)
    return pl.pallas_call(
        body, grid=(t // tm,),
        in_specs=[pl.BlockSpec((tm, w3), lambda i: (i, 0)),
                  pl.BlockSpec((SUBLANES, w3), lambda i: (jnp.maximum(i * (tm // SUBLANES) - 1, 0), 0)),
                  pl.BlockSpec((CONV_W, w3), lambda i: (0, 0))],
        out_specs=[hm, hm, pl.BlockSpec((w, tm), lambda i: (0, i)), hm],
        out_shape=[hm_shape, hm_shape, jax.ShapeDtypeStruct((w, t), F32), hm_shape],
        compiler_params=_cparams("parallel"), name="gdn_prep")(dqkv, dqkv, conv_w)


def _mm3(a, b):
    ah, al = _split2(a)
    bh, bl = _split2(b)
    return _dot(ah, bh) + _dot(ah, bl) + _dot(al, bh)


def _gdn_wy(q, k, kt, v, post_col, post_row):
    nh, t, d = q.shape
    c = math.gcd(t, GDN_CHUNK)
    span = LANES if t % LANES == 0 else t
    per = span // c

    def body(q_ref, k_ref, kt_ref, v_ref, pc_ref, pr_ref, u_ref, w_ref, qe_ref, qk_ref, kdt_ref, egl_ref):
        ri = _iota((c, c), 0)
        ci = _iota((c, c), 1)
        tril = ri >= ci
        strict = ri > ci
        lower_incl = jnp.where(tril, 1.0, 0.0).astype(BF16)
        upper_incl = jnp.where(ri <= ci, 1.0, 0.0).astype(BF16)
        eye = jnp.where(ri == ci, 1.0, 0.0)
        for s in range(per):
            rows = slice(s * c, (s + 1) * c)
            pc = pc_ref[rows, :]
            pr = pr_ref[:, rows]
            cg_col_all = _dot_3x(lower_incl, pc)
            cg_row_all = _dot_x3(pr, upper_incl)
            for h in range(nh):
                cgc = cg_col_all[:, nh + h:nh + h + 1]
                cgr = cg_row_all[nh + h:nh + h + 1, :]
                beta = pc[:, 2 * nh + h:2 * nh + h + 1]
                g_last = cgr[:, c - 1:c]
                decay = jnp.where(tril, jnp.exp(jnp.where(tril, cgc - cgr, 0.0)), 0.0)
                qh = q_ref[h, rows, :]
                kh = k_ref[h, rows, :]
                vh = v_ref[h, rows, :]
                kth = kt_ref[h * d:(h + 1) * d, rows]
                kthb = kth.astype(BF16)
                k_beta = kh * beta
                a = jnp.where(strict, _dot(k_beta.astype(BF16), kthb) * decay, 0.0)
                x = -a
                inv = eye + x
                n_sq = max(1, (c - 1).bit_length()) - 1
                for _ in range(n_sq):
                    x = _mm3(x, x)
                    inv = inv + _mm3(inv, x)
                u = _mm3(inv, vh * beta)
                wv = _mm3(inv, k_beta * jnp.exp(cgc))
                qk = jnp.where(tril, _dot(qh.astype(BF16), kthb) * decay, 0.0)
                u_ref[h, rows, :] = u
                w_ref[h, rows, :] = wv.astype(BF16)
                qe_ref[h, rows, :] = (qh * jnp.exp(cgc)).astype(BF16)
                qk_ref[h, rows, :] = qk.astype(BF16)
                kdt_ref[h * d:(h + 1) * d, rows] = (kth * jnp.exp(g_last - cgr)).astype(BF16)
                egl_ref[h, rows, :] = jnp.broadcast_to(jnp.exp(g_last), (c, d))

    hm = pl.BlockSpec((nh, span, d), lambda i: (0, i, 0))
    ktb = pl.BlockSpec((nh * d, span), lambda i: (0, i))
    return pl.pallas_call(
        body, grid=(t // span,),
        in_specs=[hm, hm, ktb, hm, pl.BlockSpec((span, LANES), lambda i: (i, 0)),
                  pl.BlockSpec((post_row.shape[0], span), lambda i: (0, i))],
        out_specs=[hm, hm, hm, hm, ktb, hm],
        out_shape=[jax.ShapeDtypeStruct((nh, t, d), F32), jax.ShapeDtypeStruct((nh, t, d), BF16),
                   jax.ShapeDtypeStruct((nh, t, d), BF16), jax.ShapeDtypeStruct((nh, t, d), BF16),
                   jax.ShapeDtypeStruct((nh * d, t), BF16), jax.ShapeDtypeStruct((nh, t, d), F32)],
        compiler_params=_cparams("parallel"), name="gdn_wy")(q, k, kt, v, post_col, post_row)


def _gdn_scan(u, w, qe, qk, kdt, egl, dz, g_gdn):
    nh, t, d = u.shape
    c = math.gcd(t, GDN_CHUNK)
    span = LANES if t % LANES == 0 else t
    per = span // c
    n = t // span

    def body(u_ref, w_ref, qe_ref, qk_ref, kdt_ref, egl_ref, dz_ref, gg_ref, o_ref, s_out_ref, s_scr, o_scr):
        i = pl.program_id(0)

        @pl.when(i == 0)
        def _():
            s_scr[...] = jnp.zeros_like(s_scr)

        for h in range(nh):
            s = s_scr[h]
            for j in range(per):
                rows = slice(j * c, (j + 1) * c)
                sb = s.astype(BF16)
                v_new = u_ref[h, rows, :] - _dot(w_ref[h, rows, :], sb)
                vb = v_new.astype(BF16)
                o = _dot(qe_ref[h, rows, :], sb) + _dot(qk_ref[h, rows, :], vb)
                s = s * egl_ref[h, rows, :] + _dot(kdt_ref[h * d:(h + 1) * d, rows], vb)
                y = o * lax.rsqrt(jnp.mean(o * o, axis=-1, keepdims=True) + EPS) * gg_ref[...]
                o_scr[rows, h * d:(h + 1) * d] = y
            s_scr[h] = s
        o_ref[...] = (o_scr[...] * _silu(dz_ref[...])).astype(o_ref.dtype)

        @pl.when(i == n - 1)
        def _():
            s_out_ref[...] = s_scr[...]

    hm = pl.BlockSpec((nh, span, d), lambda i: (0, i, 0))
    return pl.pallas_call(
        body, grid=(n,),
        in_specs=[hm, hm, hm, hm, pl.BlockSpec((nh * d, span), lambda i: (0, i)), hm,
                  pl.BlockSpec((span, nh * d), lambda i: (i, 0)), pl.BlockSpec((1, d), lambda i: (0, 0))],
        out_specs=[pl.BlockSpec((span, nh * d), lambda i: (i, 0)),
                   pl.BlockSpec((nh, d, d), lambda i: (0, 0, 0))],
        out_shape=[jax.ShapeDtypeStruct((t, nh * d), BF16), jax.ShapeDtypeStruct((nh, d, d), F32)],
        scratch_shapes=[pltpu.VMEM((nh, d, d), F32), pltpu.VMEM((span, nh * d), F32)],
        compiler_params=_cparams("arbitrary"), name="gdn_scan")(u, w, qe, qk, kdt, egl, dz, g_gdn)


def _mem_attn_core(q, mk, mv, precise):
    scale = MEM_HEAD_DIM ** -0.5
    outs = []
    for h in range(MEM_HEADS):
        sl = slice(h * MEM_HEAD_DIM, (h + 1) * MEM_HEAD_DIM)
        if precise:
            s = _dot_nt_hi(q[:, sl], mk[:, sl]) * scale
        else:
            s = _dot_nt(q[:, sl].astype(BF16), mk[:, sl]) * scale
        e = jnp.exp(s - jnp.max(s, axis=-1, keepdims=True))
        p = e / jnp.sum(e, axis=-1, keepdims=True)
        outs.append(_dot_hi(p, mv[:, sl]) if precise else _dot(p.astype(BF16), mv[:, sl]))
    return jnp.concatenate(outs, axis=-1)


def _mem_attn_prompt(y, g, wq, mk, mv, wo, *, tm=512):
    m, d = y.shape
    tm = min(tm, m)

    def body(y_ref, g_ref, wq_ref, mk_ref, mv_ref, wo_ref, o_ref):
        yv = y_ref[...]
        hb = _rms(yv, g_ref[...]).astype(BF16)
        q = _dot(hb, wq_ref[...])
        o = _mem_attn_core(q, mk_ref[...], mv_ref[...], False)
        o_ref[...] = yv + _dot(o.astype(BF16), wo_ref[...])

    full = lambda a: pl.BlockSpec(a.shape, lambda i: (0, 0))
    return pl.pallas_call(
        body, grid=(m // tm,),
        in_specs=[pl.BlockSpec((tm, d), lambda i: (i, 0)), pl.BlockSpec((1, d), lambda i: (0, 0)),
                  full(wq), full(mk), full(mv), full(wo)],
        out_specs=pl.BlockSpec((tm, d), lambda i: (i, 0)),
        out_shape=jax.ShapeDtypeStruct((m, d), F32),
        compiler_params=_cparams("parallel"), name="mem_attn_prompt")(y, g, wq, mk, mv, wo)


def _mem_attn_sample(q, mem_k, mem_v):
    b, _, w = q.shape
    n_mem = mem_k.shape[1]

    def body(q_ref, k_ref, v_ref, o_ref):
        q8 = jnp.broadcast_to(q_ref[0], (SUBLANES, w))
        o = _mem_attn_core(q8, k_ref[0], v_ref[0], True)
        o_ref[0] = o[0:1]

    return pl.pallas_call(
        body, grid=(b,),
        in_specs=[pl.BlockSpec((1, 1, w), lambda i: (i, 0, 0)),
                  pl.BlockSpec((1, n_mem, w), lambda i: (i, 0, 0)),
                  pl.BlockSpec((1, n_mem, w), lambda i: (i, 0, 0))],
        out_specs=pl.BlockSpec((1, 1, w), lambda i: (i, 0, 0)),
        out_shape=jax.ShapeDtypeStruct((b, 1, w), F32),
        compiler_params=_cparams("parallel"), name="mem_attn_sample")(q, mem_k, mem_v)


def _route(logits):
    m = logits.shape[0]
    lane = _iota((m, LANES), 1)
    big = LANES
    neg = -jnp.inf
    is_g = lane < N_GROUPS
    gl = jnp.where(is_g, logits, neg)
    gmax = jnp.max(gl, axis=1, keepdims=True)
    g_val = 1.0 / jnp.sum(jnp.where(is_g, jnp.exp(gl - gmax), 0.0), axis=1, keepdims=True)
    g_idx = jnp.min(jnp.where(is_g & (gl == gmax), lane, big), axis=1, keepdims=True)
    lo = N_GROUPS + g_idx * EXPERTS_PER_GROUP
    in_grp = (lane >= lo) & (lane < lo + EXPERTS_PER_GROUP)
    el = jnp.where(in_grp, logits, neg)
    l1 = jnp.max(el, axis=1, keepdims=True)
    i1 = jnp.min(jnp.where(in_grp & (el == l1), lane, big), axis=1, keepdims=True)
    el2 = jnp.where(lane == i1, neg, el)
    l2 = jnp.max(el2, axis=1, keepdims=True)
    i2 = jnp.min(jnp.where(in_grp & (lane != i1) & (el2 == l2), lane, big), axis=1, keepdims=True)
    p2 = jnp.exp(l2 - l1)
    w1 = g_val / (1.0 + p2)
    w2 = g_val * p2 / (1.0 + p2)
    return jnp.where(lane == i1, w1, jnp.where(lane == i2, w2, 0.0))


def _moe_dense(y, g, w_route, w_gate, w_up, w_down, *, precise=False, tm=1024, final_g=None):
    m, d = y.shape
    tm = min(tm, m)
    ne, _, ff = w_gate.shape
    xdt = F32 if precise else BF16

    def body(*refs):
        y_ref, g_ref, wr_ref, wg_ref, wu_ref, wd_ref = refs[:6]
        pos = 6
        fg_ref = None
        if final_g is not None:
            fg_ref = refs[6]
            pos = 7
        o_ref = refs[pos]
        pos += 1
        on_ref = None
        if final_g is not None:
            on_ref = refs[pos]
            pos += 1
        xn_scr, gate_scr, acc_scr = refs[pos:pos + 3]
        e = pl.program_id(1)

        @pl.when(e == 0)
        def _():
            xn = _rms(y_ref[...], g_ref[...])
            xn_scr[...] = xn.astype(xdt)
            logits = _dot_hi(xn, wr_ref[...]) if precise else _dot(xn.astype(BF16), wr_ref[...])
            gate_scr[...] = _route(logits)
            acc_scr[...] = jnp.zeros_like(acc_scr)

        xn = xn_scr[...]
        if precise:
            hg = _dot_hi(xn, wg_ref[0])
            hu = _dot_hi(xn, wu_ref[0])
            out = _dot_hi(_silu(hg) * hu, wd_ref[0])
        else:
            hg = _dot(xn, wg_ref[0])
            hu = _dot(xn, wu_ref[0])
            out = _dot((_silu(hg) * hu).astype(BF16), wd_ref[0])
        lane = _iota((tm, LANES), 1)
        ge = jnp.sum(jnp.where(lane == N_GROUPS + e, gate_scr[...], 0.0), axis=1, keepdims=True)
        acc_scr[...] += ge * out

        @pl.when(e == ne - 1)
        def _():
            res = y_ref[...] + acc_scr[...]
            o_ref[...] = res
            if on_ref is not None:
                on_ref[...] = _rms(res, fg_ref[...])

    tok = pl.BlockSpec((tm, d), lambda i, e: (i, 0))
    in_specs = [tok, pl.BlockSpec((1, d), lambda i, e: (0, 0)), pl.BlockSpec(w_route.shape, lambda i, e: (0, 0)),
                pl.BlockSpec((1, d, ff), lambda i, e: (e, 0, 0)), pl.BlockSpec((1, d, ff), lambda i, e: (e, 0, 0)),
                pl.BlockSpec((1, ff, d), lambda i, e: (e, 0, 0))]
    args = [y, g, w_route, w_gate, w_up, w_down]
    out_shape = [jax.ShapeDtypeStruct((m, d), F32)]
    out_specs = [tok]
    if final_g is not None:
        in_specs.append(pl.BlockSpec((1, d), lambda i, e: (0, 0)))
        args.append(final_g)
        out_shape.append(jax.ShapeDtypeStruct((m, d), F32))
        out_specs.append(tok)
    res = pl.pallas_call(
        body, grid=(m // tm, ne), in_specs=in_specs, out_specs=out_specs, out_shape=out_shape,
        scratch_shapes=[pltpu.VMEM((tm, d), xdt), pltpu.VMEM((tm, LANES), F32), pltpu.VMEM((tm, d), F32)],
        compiler_params=_cparams("parallel", "arbitrary"), name="moe_dense")(*args)
    return res[0] if final_g is None else res


def _heads3(x):
    return x.reshape(N_HEADS, HEAD_DIM, x.shape[-1])


def _retention_sample(q_col, k_col, v, gate, state, g_ret, inv_col, pos):
    b, w, _ = q_col.shape
    scale = HEAD_DIM ** -0.5
    half = HEAD_DIM // 2

    def body(q_ref, k_ref, v_ref, gate_ref, s_ref, gr_ref, inv_ref, o_ref, so_ref):
        row = _iota((w, 1), 0)
        first_half = (row % HEAD_DIM) < half
        ang = float(pos) * inv_ref[...]
        cos, sin = jnp.cos(ang), jnp.sin(ang)

        def rot(x):
            return x * cos + jnp.where(first_half, -pltpu.roll(x, w - half, 0), pltpu.roll(x, half, 0)) * sin

        q = rot(q_ref[0])
        k = rot(k_ref[0]) * scale
        gamma = 1.0 - jnp.exp2(-5.0 - (row // HEAD_DIM).astype(F32))
        s = s_ref[0]
        vv = v_ref[0]
        qk = jnp.sum(_heads3(q * k), axis=1)
        qs = jnp.sum(_heads3(q * s * gamma), axis=1)
        o = qk * vv + qs
        v_rows = jnp.broadcast_to(vv[:, None, :], (N_HEADS, HEAD_DIM, HEAD_DIM)).reshape(w, HEAD_DIM)
        so_ref[0] = s * gamma + k * v_rows
        mu = jnp.mean(o, axis=-1, keepdims=True)
        dlt = o - mu
        var = jnp.mean(dlt * dlt, axis=-1, keepdims=True)
        o_ref[0] = dlt * lax.rsqrt(var + EPS) * gr_ref[...] * _silu(gate_ref[0])

    col = pl.BlockSpec((1, w, 1), lambda i: (i, 0, 0))
    hd = pl.BlockSpec((1, N_HEADS, HEAD_DIM), lambda i: (i, 0, 0))
    st = pl.BlockSpec((1, w, HEAD_DIM), lambda i: (i, 0, 0))
    return pl.pallas_call(
        body, grid=(b,),
        in_specs=[col, col, hd, hd, st, pl.BlockSpec((N_HEADS, HEAD_DIM), lambda i: (0, 0)),
                  pl.BlockSpec((w, 1), lambda i: (0, 0))],
        out_specs=[hd, st],
        out_shape=[jax.ShapeDtypeStruct((b, N_HEADS, HEAD_DIM), F32), jax.ShapeDtypeStruct((b, w, HEAD_DIM), F32)],
        compiler_params=_cparams("parallel"), name="retention_sample")(q_col, k_col, v, gate, state, g_ret, inv_col)


def _gdn_sample(xqk_col, bufqk_col, cwqk_col, xv, bufv, cwv, dz, small_col, bias_col, alog_col, state, g_gdn):
    b = xqk_col.shape[0]
    w = N_HEADS * HEAD_DIM
    scale = HEAD_DIM ** -0.5
    r = small_col.shape[1]

    def body(x_ref, bq_ref, cq_ref, xv_ref, bv_ref, cv_ref, dz_ref, sm_ref, bc_ref, ac_ref, s_ref, gg_ref,
             o_ref, so_ref):
        yqk = x_ref[0] * cq_ref[CONV_W - 1]
        yv = xv_ref[0] * cv_ref[CONV_W - 1]
        for j in range(CONV_W - 1):
            yqk = yqk + bq_ref[0, j] * cq_ref[j]
            yv = yv + bv_ref[0, j] * cv_ref[j]
        yqk = _silu(yqk)
        vv = _silu(yv)

        def l2n(a):
            a3 = _heads3(a)
            ss = jnp.sum(a3 * a3, axis=1, keepdims=True)
            return a3 * lax.rsqrt(ss + EPS)

        q3 = l2n(yqk[:w]) * scale
        k3 = l2n(yqk[w:])
        post = _gate_math(sm_ref[0], bc_ref[...], ac_ref[...], _iota((r, 1), 0))
        g = post[N_HEADS:2 * N_HEADS]
        beta = post[2 * N_HEADS:3 * N_HEADS]
        a = jnp.exp(g)
        s3 = _heads3(s_ref[0])
        ks = jnp.sum(k3 * s3, axis=1)
        v_new = beta * vv - (beta * a) * ks
        o = a * jnp.sum(q3 * s3, axis=1) + jnp.sum(q3 * k3, axis=1) * v_new
        s_new = s3 * a[:, :, None] + k3 * v_new[:, None, :]
        so_ref[0] = s_new.reshape(w, HEAD_DIM)
        y = o * lax.rsqrt(jnp.mean(o * o, axis=-1, keepdims=True) + EPS) * gg_ref[...]
        o_ref[0] = y * _silu(dz_ref[0])

    hd = pl.BlockSpec((1, N_HEADS, HEAD_DIM), lambda i: (i, 0, 0))
    st = pl.BlockSpec((1, w, HEAD_DIM), lambda i: (i, 0, 0))
    return pl.pallas_call(
        body, grid=(b,),
        in_specs=[pl.BlockSpec((1, 2 * w, 1), lambda i: (i, 0, 0)),
                  pl.BlockSpec((1, CONV_W - 1, 2 * w, 1), lambda i: (i, 0, 0, 0)),
                  pl.BlockSpec((CONV_W, 2 * w, 1), lambda i: (0, 0, 0)),
                  hd, pl.BlockSpec((1, CONV_W - 1, N_HEADS, HEAD_DIM), lambda i: (i, 0, 0, 0)),
                  pl.BlockSpec((CONV_W, N_HEADS, HEAD_DIM), lambda i: (0, 0, 0)),
                  hd, pl.BlockSpec((1, r, 1), lambda i: (i, 0, 0)),
                  pl.BlockSpec((r, 1), lambda i: (0, 0)), pl.BlockSpec((r, 1), lambda i: (0, 0)),
                  st, pl.BlockSpec((1, HEAD_DIM), lambda i: (0, 0))],
        out_specs=[hd, st],
        out_shape=[jax.ShapeDtypeStruct((b, N_HEADS, HEAD_DIM), F32), jax.ShapeDtypeStruct((b, w, HEAD_DIM), F32)],
        compiler_params=_cparams("parallel"), name="gdn_sample")(
            xqk_col, bufqk_col, cwqk_col, xv, bufv, cwv, dz, small_col, bias_col, alog_col, state, g_gdn)


PAGES_PER_STEP = 8


def _paged_decode(mode, q_col, cache_k, cache_v, page_table, page_base, *, k_new=None, v_new=None,
                  cache_logf=None, logf_new=None):
    fox = mode == "fox"
    b, w, _ = q_col.shape
    n_pages = page_table.shape[1]
    page = cache_k.shape[1]
    pps = math.gcd(n_pages, PAGES_PER_STEP)
    ng = n_pages // pps
    qscale = HEAD_DIM ** -0.5 * LOG2E
    groups = page // SUBLANES

    def body(*refs):
        pt_ref = refs[0]
        del pt_ref
        q_ref = refs[1]
        pos = 2
        k_refs = refs[pos:pos + pps]
        pos += pps
        v_refs = refs[pos:pos + pps]
        pos += pps
        if fox:
            lf_refs = refs[pos:pos + pps]
            pos += pps
            kn_ref, vn_ref, lfn_ref = refs[pos:pos + 3]
            pos += 3
        o_ref = refs[pos]
        pos += 1
        qm_scr, acc_scr, carry_scr = refs[pos:pos + 3]
        pos += 3
        if fox:
            m_scr, l_scr, pad_scr = refs[pos:pos + 3]
        j = pl.program_id(1)
        expand = jnp.where(_iota((LANES, w), 0) == _iota((LANES, w), 1) // HEAD_DIM, 1.0, 0.0).astype(BF16)
        later = jnp.where(_iota((page, page), 1) > _iota((page, page), 0), 1.0, 0.0).astype(BF16)

        @pl.when(j == 0)
        def _():
            head_of_row = _iota((w, LANES), 0) // HEAD_DIM
            qm = jnp.where(head_of_row == _iota((w, LANES), 1), q_ref[0] * qscale, 0.0).astype(BF16)
            qm_scr[...] = qm
            if fox:
                kn = jnp.broadcast_to(kn_ref[0], (SUBLANES, w)).astype(BF16)
                m_scr[...] = _dot(kn, qm)[0:1]
                l_scr[...] = jnp.ones_like(l_scr)
                acc_scr[...] = jnp.where(_iota((SUBLANES, w), 0) == 0, vn_ref[0], 0.0)
                carry_scr[...] = lfn_ref[0] * LOG2E
                pad_scr[...] = jnp.zeros_like(pad_scr)
            else:
                acc_scr[...] = jnp.zeros_like(acc_scr)
                carry_scr[...] = jnp.zeros_like(carry_scr)

        qm = qm_scr[...]
        carry = carry_scr[...]
        if not fox:
            acc = acc_scr[...]
            for p in range(pps):
                z = _dot(k_refs[p][0].astype(BF16), qm)
                sp = jnp.maximum(z, 0.0) + jnp.log(1.0 + jnp.exp2(-jnp.abs(z))) * LOG2E
                hi, lo = _split2(sp)
                after = _dot(later, hi) + _dot(later, lo)
                wgt = jnp.exp2(z - sp - after - carry)
                carry = carry + jnp.sum(sp, axis=0, keepdims=True)
                wx = _dot(wgt.astype(BF16), expand)
                acc = acc + jnp.sum((wx * v_refs[p][0]).reshape(groups, SUBLANES, w), axis=0)
            acc_scr[...] = acc
            carry_scr[...] = carry
        else:
            logits = []
            for p in range(pps):
                pad_scr[p, :, 0:N_HEADS] = lf_refs[p][0]
                lf = pad_scr[p] * LOG2E
                z = _dot(k_refs[p][0].astype(BF16), qm)
                logits.append(z + (_dot_3x(later, lf) + carry))
                carry = carry + jnp.sum(lf, axis=0, keepdims=True)
            carry_scr[...] = carry
            m_old = m_scr[...]
            m_new = m_old
            for s in logits:
                m_new = jnp.maximum(m_new, jnp.max(s, axis=0, keepdims=True))
            alpha = jnp.exp2(m_old - m_new)
            l_new = alpha * l_scr[...]
            alpha_x = _dot_x3(jnp.broadcast_to(alpha, (SUBLANES, LANES)), expand)[0:1]
            acc = acc_scr[...] * alpha_x
            for p in range(pps):
                pr = jnp.exp2(logits[p] - m_new)
                l_new = l_new + jnp.sum(pr, axis=0, keepdims=True)
                wx = _dot(pr.astype(BF16), expand)
                acc = acc + jnp.sum((wx * v_refs[p][0]).reshape(groups, SUBLANES, w), axis=0)
            acc_scr[...] = acc
            m_scr[...] = m_new
            l_scr[...] = l_new

        @pl.when(j == ng - 1)
        def _():
            tot = jnp.sum(acc_scr[...], axis=0, keepdims=True)
            if fox:
                l_x = _dot_x3(jnp.broadcast_to(l_scr[...], (SUBLANES, LANES)), expand)[0:1]
                tot = tot / l_x
            o_ref[0] = tot

    def page_map(p):
        def index_map(i, j, pt):
            return (page_base + pt[i, (ng - 1 - j) * pps + (pps - 1 - p)], 0, 0)
        return index_map

    per_seq = lambda i, j, pt: (i, 0, 0)
    in_specs = [pl.BlockSpec((1, w, 1), per_seq)]
    in_specs += [pl.BlockSpec((1, page, w), page_map(p)) for p in range(pps)]
    in_specs += [pl.BlockSpec((1, page, w), page_map(p)) for p in range(pps)]
    args = [q_col] + [cache_k] * pps + [cache_v] * pps
    scratch = [pltpu.VMEM((w, LANES), BF16), pltpu.VMEM((SUBLANES, w), F32), pltpu.VMEM((1, LANES), F32)]
    if fox:
        in_specs += [pl.BlockSpec((1, page, N_HEADS), page_map(p)) for p in range(pps)]
        in_specs += [pl.BlockSpec((1, 1, w), per_seq), pl.BlockSpec((1, 1, w), per_seq),
                     pl.BlockSpec((1, 1, LANES), per_seq)]
        args += [cache_logf] * pps + [k_new, v_new, logf_new]
        scratch += [pltpu.VMEM((1, LANES), F32), pltpu.VMEM((1, LANES), F32), pltpu.VMEM((pps, page, LANES), F32)]
    return pl.pallas_call(
        body,
        grid_spec=pltpu.PrefetchScalarGridSpec(
            num_scalar_prefetch=1, grid=(b, ng), in_specs=in_specs,
            out_specs=pl.BlockSpec((1, 1, w), per_seq), scratch_shapes=scratch),
        out_shape=jax.ShapeDtypeStruct((b, 1, w), F32),
        compiler_params=_cparams("parallel", "arbitrary"), name="paged_decode_" + mode)(page_table, *args)


def kernel(x_prompt, x_sample, state_ret, cache_sb_k, cache_sb_v, cache_fox_k, cache_fox_v, cache_fox_logf, state_gdn, state_conv, cache_mem_k, cache_mem_v, page_table, mem_prompt, g_mix, ev_w_in, ev_g_ret, ev_w_out, od_w_in, od_b_forget, od_conv_w, od_a_log, od_dt_bias, od_g_gdn, od_w_out, g_xattn, g_mem, w_mq, w_mk, w_mv, w_mo, g_ffn, w_route_group, w_route_expert, w_exp_gate, w_exp_up, w_exp_down, g_final):
    bp, t, d = x_prompt.shape
    bs, ts, _ = x_sample.shape
    assert bp == 1 and ts == 1 and t >= CONV_W - 1
    depth = g_mix.shape[0]
    n_pages = page_table.shape[1]
    n_phys, page = cache_sb_k.shape[1], cache_sb_k.shape[2]
    past_len = n_pages * page
    w = MIX_HALF
    nh, hd = N_HEADS, HEAD_DIM
    attn_scale = hd ** -0.5 * LOG2E
    f32o = [(F32, 1.0)]
    both = [(F32, 1.0), (BF16, 1.0)]
    row2 = lambda v: v.reshape(1, -1)

    half = hd // 2
    inv = ROPE_BASE ** (-jnp.arange(half, dtype=F32) / half)
    inv_lane = jnp.tile(inv, LANES // half)[None]
    inv_col = jnp.tile(inv, w // half)[:, None]

    yp = x_prompt[0]
    ys = x_sample[:, 0]
    y_prompt = y_sample = None
    ret_p, sbk_p, sbv_p, foxk_p, foxv_p, foxl_p, gdn_p, conv_p, memk_p, memv_p = ([] for _ in range(10))
    ret_s, sbk_s, sbv_s, foxk_s, foxv_s, foxl_s, gdn_s, conv_s = ([] for _ in range(8))

    for layer in range(depth):
        i = layer // 2
        g_in = row2(g_mix[layer])
        if layer % 2 == 0:
            w_in = ev_w_in[i]
            w_out = ev_w_out[i]
            outs = [(0, w, f32o), (w, w, f32o), (2 * w, w, f32o), (3 * w, w, f32o),
                    (4 * w, w, [(BF16, attn_scale)]), (5 * w, w, both), (6 * w, w, both)]
            rq, rk, rv, rg, sqb, sk, skb, sv, svb = _norm_proj(yp, g_in, w_in.astype(BF16), outs)
            o_ret, s_ret = _retention_prompt(rq, rk, rv, rg, row2(ev_g_ret[i]), inv_lane)
            o_sb = _sb_prompt(sqb, skb, svb)
            w_out_b = w_out.astype(BF16)
            yp = _resid_proj(yp, [o_ret, o_sb], [w_out_b[:w], w_out_b[w:]])
            ret_p.append(s_ret[None])
            sbk_p.append(sk.reshape(1, t, nh, hd))
            sbv_p.append(sv.reshape(1, t, nh, hd))
            outs = [(c * w, w, f32o) for c in range(7)]
            rq, rk, rv, rg, sq, sk, sv = _norm_proj(ys, g_in, w_in, outs, precise=True)
            o_ret, s_new = _retention_sample(
                rq.reshape(bs, w, 1), rk.reshape(bs, w, 1), rv.reshape(bs, nh, hd), rg.reshape(bs, nh, hd),
                state_ret[i].reshape(bs, w, hd), ev_g_ret[i].reshape(nh, hd), inv_col, past_len)
            o_sb = _paged_decode("sb", sq.reshape(bs, w, 1), cache_sb_k.reshape(-1, page, w),
                                 cache_sb_v.reshape(-1, page, w), page_table, i * n_phys)
            ys = _resid_proj(ys, [o_ret.reshape(bs, w), o_sb.reshape(bs, w)], [w_out[:w], w_out[w:]], precise=True)
            ret_s.append(s_new.reshape(bs, nh, hd, hd))
            sbk_s.append(sk.reshape(bs, 1, nh, hd))
            sbv_s.append(sv.reshape(bs, 1, nh, hd))
        else:
            w_full = od_w_in[i]
            w_out = od_w_out[i]
            c_fl = 3 * w
            c_qkv = c_fl + nh
            c_dz = c_qkv + 3 * w
            c_da = c_dz + w
            n_small = 3 * nh
            small_w = jnp.concatenate([w_full[:, c_fl:c_qkv], w_full[:, c_da:c_da + 2 * nh],
                                       jnp.zeros((d, LANES - n_small), F32)], axis=1)
            w_in = jnp.concatenate([w_full[:, :c_fl], w_full[:, c_qkv:c_da], small_w], axis=1)
            o_dqkv, o_dz, o_small = 3 * w, 6 * w, 7 * w
            zeros8 = jnp.zeros((nh,), F32)
            bias = jnp.concatenate([od_b_forget[i], od_dt_bias[i], zeros8])
            alog = jnp.concatenate([zeros8, od_a_log[i], zeros8])
            n_rows = 4 * nh
            bias_lane = jnp.pad(bias, (0, LANES - n_small))[None]
            alog_lane = jnp.pad(alog, (0, LANES - n_small))[None]
            bias_col = jnp.pad(bias, (0, n_rows - n_small))[:, None]
            alog_col = jnp.pad(alog, (0, n_rows - n_small))[:, None]
            outs = [(0, w, [(BF16, attn_scale)]), (w, w, both), (2 * w, w, both),
                    (o_dqkv, 3 * w, f32o), (o_dz, w, f32o), (o_small, LANES, f32o)]
            w_in_b = w_in.astype(BF16)
            small_wt = small_w[:, :n_rows].T.astype(BF16)
            fqb, fk, fkb, fv, fvb, dqkv, dz, small_col, small_row = _norm_proj(yp, g_in, w_in_b, outs, wt=small_wt)
            post_col, post_row = _gates(small_col, small_row, bias_lane, alog_lane, bias_col, alog_col)
            n_chunks = t // LANES
            cum_row = _cum_forget(post_row[:nh].reshape(nh * n_chunks, LANES), n_chunks).reshape(nh, t)
            o_fox = _fox_prompt(fqb, fkb, fvb, cum_row.T, cum_row)
            gq, gk, gkt, gv = _gdn_prep(dqkv, od_conv_w[i])
            wy = _gdn_wy(gq, gk, gkt, gv, post_col, post_row)
            o_gdn, s_gdn = _gdn_scan(*wy, dz, row2(od_g_gdn[i]))
            w_out_b = w_out.astype(BF16)
            yp = _resid_proj(yp, [o_fox, o_gdn], [w_out_b[:w], w_out_b[w:]])
            foxk_p.append(fk.reshape(1, t, nh, hd))
            foxv_p.append(fv.reshape(1, t, nh, hd))
            foxl_p.append(post_col[:, :nh].reshape(1, t, nh))
            gdn_p.append(s_gdn[None])
            conv_p.append(dqkv[t - (CONV_W - 1):][None])
            outs = [(0, w, f32o), (w, w, f32o), (2 * w, w, f32o), (o_dqkv, 3 * w, f32o), (o_dz, w, f32o),
                    (o_small, LANES, f32o)]
            fq, fk, fv, dqkv, dz, small = _norm_proj(ys, g_in, w_in, outs, precise=True)
            post_s, _ = _gates(small, small[:, :n_rows].T, bias_lane, alog_lane, bias_col, alog_col)
            o_fox = _paged_decode(
                "fox", fq.reshape(bs, w, 1), cache_fox_k.reshape(-1, page, w), cache_fox_v.reshape(-1, page, w),
                page_table, i * n_phys, k_new=fk.reshape(bs, 1, w), v_new=fv.reshape(bs, 1, w),
                cache_logf=cache_fox_logf.reshape(-1, page, nh), logf_new=post_s.reshape(bs, 1, LANES))
            buf = state_conv[i]
            cw = od_conv_w[i]
            o_gdn, s_new = _gdn_sample(
                dqkv[:, :2 * w].reshape(bs, 2 * w, 1), buf[:, :, :2 * w].reshape(bs, CONV_W - 1, 2 * w, 1),
                cw[:, :2 * w].reshape(CONV_W, 2 * w, 1), dqkv[:, 2 * w:].reshape(bs, nh, hd),
                buf[:, :, 2 * w:].reshape(bs, CONV_W - 1, nh, hd), cw[:, 2 * w:].reshape(CONV_W, nh, hd),
                dz.reshape(bs, nh, hd), small[:, :n_rows].reshape(bs, n_rows, 1), bias_col, alog_col,
                state_gdn[i].reshape(bs, w, hd), row2(od_g_gdn[i]))
            ys = _resid_proj(ys, [o_fox.reshape(bs, w), o_gdn.reshape(bs, w)], [w_out[:w], w_out[w:]], precise=True)
            foxk_s.append(fk.reshape(bs, 1, nh, hd))
            foxv_s.append(fv.reshape(bs, 1, nh, hd))
            foxl_s.append(post_s[:, :nh].reshape(bs, 1, nh))
            gdn_s.append(s_new.reshape(bs, nh, hd, hd))
            conv_s.append(jnp.concatenate([buf[:, 1:], dqkv[:, None]], axis=1))

        n_mem = mem_prompt.shape[1]
        mw = MEM_HEADS * MEM_HEAD_DIM
        w_kv = jnp.concatenate([w_mk[layer], w_mv[layer]], axis=1).astype(BF16)
        mk, mkb, mv, mvb = _norm_proj(mem_prompt[0], row2(g_mem[layer]), w_kv, [(0, mw, both), (mw, mw, both)])
        memk_p.append(mk.reshape(1, n_mem, MEM_HEADS, MEM_HEAD_DIM))
        memv_p.append(mv.reshape(1, n_mem, MEM_HEADS, MEM_HEAD_DIM))
        g_x = row2(g_xattn[layer])
        yp = _mem_attn_prompt(yp, g_x, w_mq[layer].astype(BF16), mkb, mvb, w_mo[layer].astype(BF16))
        (q_s,) = _norm_proj(ys, g_x, w_mq[layer], [(0, mw, f32o)], precise=True)
        o_mem = _mem_attn_sample(q_s.reshape(bs, 1, mw), cache_mem_k[layer].reshape(bs, n_mem, mw),
                                 cache_mem_v[layer].reshape(bs, n_mem, mw))
        ys = _resid_proj(ys, [o_mem.reshape(bs, mw)], [w_mo[layer]], precise=True)

        w_route = jnp.concatenate([w_route_group[layer], w_route_expert[layer],
                                   jnp.zeros((d, LANES - N_GROUPS - N_EXPERTS), F32)], axis=1)
        g_f = row2(g_ffn[layer])
        fin = row2(g_final) if layer == depth - 1 else None
        res_p = _moe_dense(yp, g_f, w_route.astype(BF16), w_exp_gate[layer].astype(BF16),
                           w_exp_up[layer].astype(BF16), w_exp_down[layer].astype(BF16), final_g=fin)
        res_s = _moe_dense(ys, g_f, w_route, w_exp_gate[layer], w_exp_up[layer], w_exp_down[layer],
                           precise=True, final_g=fin)
        if fin is None:
            yp, ys = res_p, res_s
        else:
            (yp, y_prompt), (ys, y_sample) = res_p, res_s

    return (y_prompt[None], y_sample[:, None],
            jnp.stack(ret_p), jnp.stack(sbk_p), jnp.stack(sbv_p),
            jnp.stack(foxk_p), jnp.stack(foxv_p), jnp.stack(foxl_p), jnp.stack(gdn_p), jnp.stack(conv_p),
            jnp.stack(memk_p), jnp.stack(memv_p),
            jnp.stack(ret_s), jnp.stack(sbk_s), jnp.stack(sbv_s),
            jnp.stack(foxk_s), jnp.stack(foxv_s), jnp.stack(foxl_s), jnp.stack(gdn_s), jnp.stack(conv_s))
```

```python
import functools
import math

import jax
import jax.numpy as jnp
from jax import lax
from jax.experimental import pallas as pl
from jax.experimental.pallas import tpu as pltpu

F32 = jnp.float32
BF16 = jnp.bfloat16
I32 = jnp.int32

HEAD_DIM = 64
N_HEADS = 8
MIX_HALF = N_HEADS * HEAD_DIM
MEM_HEADS = 4
MEM_HEAD_DIM = 128
N_GROUPS = 4
EXPERTS_PER_GROUP = 4
N_EXPERTS = N_GROUPS * EXPERTS_PER_GROUP
CONV_W = 4
RET_CHUNK = 128
GDN_CHUNK = 64
GDN_INV_BASE = 8
ROPE_BASE = 10000.0
EPS = 1e-6
LOG2E = 1.4426950408889634
LN2 = 0.6931471805599453

LANES = 128
SUBLANES = 8
VMEM_LIMIT_BYTES = 56 * 1024 * 1024

HIGHEST = lax.Precision.HIGHEST


def _cparams(*sem):
    return pltpu.CompilerParams(dimension_semantics=sem, vmem_limit_bytes=VMEM_LIMIT_BYTES)


def _dot(a, b):
    return jnp.dot(a, b, preferred_element_type=F32)


def _dot_nt(a, b):
    return lax.dot_general(a, b, (((1,), (1,)), ((), ())), preferred_element_type=F32)


def _dot_hi(a, b):
    return jnp.dot(a, b, preferred_element_type=F32, precision=HIGHEST)


def _dot_nt_hi(a, b):
    return lax.dot_general(a, b, (((1,), (1,)), ((), ())), preferred_element_type=F32, precision=HIGHEST)


def _split2(x):
    hi = x.astype(BF16)
    lo = (x - hi.astype(F32)).astype(BF16)
    return hi, lo


def _split3(x):
    hi = x.astype(BF16)
    r = x - hi.astype(F32)
    mid = r.astype(BF16)
    lo = (r - mid.astype(F32)).astype(BF16)
    return hi, mid, lo


def _dot_x3(x, m_bf16):
    hi, mid, lo = _split3(x)
    return _dot(hi, m_bf16) + _dot(mid, m_bf16) + _dot(lo, m_bf16)


def _dot_3x(m_bf16, x):
    hi, mid, lo = _split3(x)
    return _dot(m_bf16, hi) + _dot(m_bf16, mid) + _dot(m_bf16, lo)


def _rms(x, g):
    return x * lax.rsqrt(jnp.mean(x * x, axis=-1, keepdims=True) + EPS) * g


def _silu(x):
    return x / (1.0 + jnp.exp(-x))


def _softplus(x):
    return jnp.maximum(x, 0.0) + jnp.log(1.0 + jnp.exp(-jnp.abs(x)))


def _iota(shape, dim):
    return lax.broadcasted_iota(I32, shape, dim)


def _head_block_ones(n, dtype=BF16):
    r = _iota((n, n), 0) // HEAD_DIM
    c = _iota((n, n), 1) // HEAD_DIM
    return jnp.where(r == c, 1.0, 0.0).astype(dtype)


def _norm_proj(x, g, w, outs, *, wt=None, precise=False, tm=512):
    m, d = x.shape
    tm = min(tm, m)
    assert m % tm == 0
    n_out = sum(len(o[2]) for o in outs)

    def body(*refs):
        x_ref, g_ref, w_ref = refs[:3]
        pos = 3
        wt_ref = None
        if wt is not None:
            wt_ref = refs[3]
            pos = 4
        o_refs = refs[pos:]
        xn = _rms(x_ref[...], g_ref[...])
        xb = xn if precise else xn.astype(BF16)
        k = 0
        for start, width, variants in outs:
            wslab = w_ref[:, start:start + width]
            r = _dot_hi(xb, wslab) if precise else _dot(xb, wslab)
            for dtype, scale in variants:
                o_refs[k][...] = (r if scale == 1.0 else r * scale).astype(dtype)
                k += 1
        if wt_ref is not None:
            o_refs[k][...] = _dot_nt_hi(wt_ref[...], xb) if precise else _dot_nt(wt_ref[...], xb)

    in_specs = [pl.BlockSpec((tm, d), lambda i: (i, 0)),
                pl.BlockSpec((1, d), lambda i: (0, 0)),
                pl.BlockSpec(w.shape, lambda i: (0, 0))]
    args = [x, g, w]
    if wt is not None:
        in_specs.append(pl.BlockSpec(wt.shape, lambda i: (0, 0)))
        args.append(wt)
    out_shape, out_specs = [], []
    for start, width, variants in outs:
        for dtype, _ in variants:
            out_shape.append(jax.ShapeDtypeStruct((m, width), dtype))
            out_specs.append(pl.BlockSpec((tm, width), lambda i: (i, 0)))
    if wt is not None:
        out_shape.append(jax.ShapeDtypeStruct((wt.shape[0], m), F32))
        out_specs.append(pl.BlockSpec((wt.shape[0], tm), lambda i: (0, i)))
    res = pl.pallas_call(
        body, grid=(m // tm,), in_specs=in_specs, out_specs=out_specs, out_shape=out_shape,
        compiler_params=_cparams("parallel"), name="norm_proj")(*args)
    assert len(res) == n_out + (wt is not None)
    return res


def _resid_proj(res, a_list, w_list, *, precise=False, tm=512, final_g=None):
    m, d = res.shape
    tm = min(tm, m)
    n = len(a_list)

    def body(*refs):
        r_ref = refs[0]
        a_refs = refs[1:1 + n]
        w_refs = refs[1 + n:1 + 2 * n]
        pos = 1 + 2 * n
        acc = r_ref[...]
        for a_ref, w_ref in zip(a_refs, w_refs):
            acc = acc + (_dot_hi(a_ref[...], w_ref[...]) if precise else _dot(a_ref[...], w_ref[...]))
        if final_g is None:
            refs[pos][...] = acc
        else:
            refs[pos + 1][...] = acc
            refs[pos + 2][...] = _rms(acc, refs[pos][...])

    in_specs = [pl.BlockSpec((tm, d), lambda i: (i, 0))]
    in_specs += [pl.BlockSpec((tm, a.shape[1]), lambda i: (i, 0)) for a in a_list]
    in_specs += [pl.BlockSpec(w.shape, lambda i: (0, 0)) for w in w_list]
    args = [res, *a_list, *w_list]
    out_shape = [jax.ShapeDtypeStruct((m, d), F32)]
    out_specs = [pl.BlockSpec((tm, d), lambda i: (i, 0))]
    if final_g is not None:
        in_specs.append(pl.BlockSpec((1, d), lambda i: (0, 0)))
        args.append(final_g)
        out_shape.append(jax.ShapeDtypeStruct((m, d), F32))
        out_specs.append(pl.BlockSpec((tm, d), lambda i: (i, 0)))
    out = pl.pallas_call(
        body, grid=(m // tm,), in_specs=in_specs, out_specs=out_specs, out_shape=out_shape,
        compiler_params=_cparams("parallel"), name="resid_proj")(*args)
    return out[0] if final_g is None else out


def _rope_tables(pos, inv_lane):
    ang = pos * inv_lane
    return jnp.cos(ang), jnp.sin(ang)


def _rotary_lanes(x, cos, sin, first_half):
    w = x.shape[-1]
    reps = w // LANES
    c = jnp.concatenate([cos] * reps, axis=-1) if reps > 1 else cos
    s = jnp.concatenate([sin] * reps, axis=-1) if reps > 1 else sin
    x_up = pltpu.roll(x, w - HEAD_DIM // 2, 1)
    x_dn = pltpu.roll(x, HEAD_DIM // 2, 1)
    return x * c + jnp.where(first_half, -x_up, x_dn) * s


def _retention_prompt(rq, rk, rv, rg, g_ret, inv_lane):
    t, w = rq.shape
    c = math.gcd(t, RET_CHUNK)
    n = t // c
    scale = HEAD_DIM ** -0.5

    def body(q_ref, k_ref, v_ref, gate_ref, gret_ref, inv_ref, o_ref, s_out_ref,
             s_scr, dec_scr, qd_scr, kd_scr):
        i = pl.program_id(0)
        lane = _iota((c, w), 1)
        head_of_lane = lane // HEAD_DIM
        first_half = (lane % HEAD_DIM) < (HEAD_DIM // 2)
        lg_lane = jnp.log(1.0 - jnp.exp2(-5.0 - head_of_lane.astype(F32)))
        row = _iota((c, w), 0).astype(F32)

        @pl.when(i == 0)
        def _():
            s_scr[...] = jnp.zeros_like(s_scr)
            qd_scr[...] = jnp.exp((row + 1.0) * lg_lane)
            kd_scr[...] = jnp.exp((c - 1.0 - row) * lg_lane)
            ri = _iota((c, c), 0)
            ci = _iota((c, c), 1)
            diff = (ri - ci).astype(F32)
            for h in range(N_HEADS):
                lg_h = math.log(1.0 - 2.0 ** (-5.0 - h))
                dec_scr[h] = jnp.where(ri >= ci, jnp.exp(jnp.where(ri >= ci, diff, 0.0) * lg_h), 0.0)

        pos = (i * c).astype(F32) + _iota((c, 1), 0).astype(F32)
        cos, sin = _rope_tables(pos, inv_ref[...])
        q = _rotary_lanes(q_ref[...], cos, sin, first_half)
        k = _rotary_lanes(k_ref[...], cos, sin, first_half) * scale
        qb = q.astype(BF16)
        kb = k.astype(BF16)
        vb = v_ref[...].astype(BF16)
        s_old = s_scr[...]
        o = _dot(qb, s_old.astype(BF16)) * qd_scr[...]
        hms = [head_of_lane == h for h in range(N_HEADS)]
        scs = [_dot_nt(jnp.where(hm, qb, jnp.zeros_like(qb)), kb) for hm in hms]
        ohs = [_dot((scs[h] * dec_scr[h]).astype(BF16), vb) for h in range(N_HEADS)]
        for h in range(N_HEADS):
            o = o + jnp.where(hms[h], ohs[h], 0.0)
        kd = (k * kd_scr[...]).astype(BF16)
        upd = lax.dot_general(kd, vb, (((0,), (0,)), ((), ())), preferred_element_type=F32)
        blk = (_iota((w, w), 0) // HEAD_DIM) == (_iota((w, w), 1) // HEAD_DIM)
        cd = jnp.exp(float(c) * jnp.log(1.0 - jnp.exp2(-5.0 - (_iota((1, w), 1) // HEAD_DIM).astype(F32))))
        s_scr[...] = s_old * cd + jnp.where(blk, upd, 0.0)
        ones = _head_block_ones(w)
        inv_n = 1.0 / HEAD_DIM
        hi, lo = _split2(o)
        mu = (_dot(hi, ones) + _dot(lo, ones)) * inv_n
        dlt = o - mu
        hi, lo = _split2(dlt * dlt)
        var = (_dot(hi, ones) + _dot(lo, ones)) * inv_n
        y = dlt * lax.rsqrt(var + EPS) * gret_ref[...]
        o_ref[...] = (y * _silu(gate_ref[...])).astype(o_ref.dtype)

        @pl.when(i == n - 1)
        def _():
            s_fin = s_scr[...]
            for h in range(N_HEADS):
                s_out_ref[h] = s_fin[h * HEAD_DIM:(h + 1) * HEAD_DIM, h * HEAD_DIM:(h + 1) * HEAD_DIM]

    tok = pl.BlockSpec((c, w), lambda i: (i, 0))
    o, s = pl.pallas_call(
        body, grid=(n,),
        in_specs=[tok, tok, tok, tok, pl.BlockSpec((1, w), lambda i: (0, 0)),
                  pl.BlockSpec((1, LANES), lambda i: (0, 0))],
        out_specs=[tok, pl.BlockSpec((N_HEADS, HEAD_DIM, HEAD_DIM), lambda i: (0, 0, 0))],
        out_shape=[jax.ShapeDtypeStruct((t, w), BF16),
                   jax.ShapeDtypeStruct((N_HEADS, HEAD_DIM, HEAD_DIM), F32)],
        scratch_shapes=[pltpu.VMEM((w, w), F32), pltpu.VMEM((N_HEADS, c, c), F32),
                        pltpu.VMEM((c, w), F32), pltpu.VMEM((c, w), F32)],
        compiler_params=_cparams("arbitrary"), name="retention_prompt")(rq, rk, rv, rg, g_ret, inv_lane)
    return o, s


ATTN_BLOCKS = (512, 256, 128)
SB_CUM_BLOCK = 256


def _attn_block(t):
    for b in ATTN_BLOCKS:
        if t % b == 0:
            return b
    raise ValueError(f"sequence length {t} must be a multiple of {ATTN_BLOCKS[-1]}")


SOFTPLUS2_LINEAR_ABOVE = 60.0


def _softplus2(z):
    return jnp.where(z > SOFTPLUS2_LINEAR_ABOVE, z, jnp.log(1.0 + jnp.exp2(z)) * LOG2E)


def _sb_prompt(qb, kb, vb):
    t, w = qb.shape
    blk = _attn_block(t)
    cb = min(blk, SB_CUM_BLOCK)
    nblk = blk // cb
    nq = t // blk
    n_pairs = w // LANES

    def body(q_ref, k_ref, v_ref, o_ref, acc_scr, c_scr):
        qi = pl.program_id(1)
        q = q_ref[...]
        lane = _iota((blk, LANES), 1)
        zero = jnp.zeros_like(q)
        q_heads = (jnp.where(lane < HEAD_DIM, q, zero), jnp.where(lane >= HEAD_DIM, q, zero))
        incl = jnp.where(_iota((2 * cb, cb), 0) % cb >= _iota((2 * cb, cb), 1), 1.0, 0.0).astype(BF16)
        acc_scr[...] = jnp.zeros_like(acc_scr)
        c_scr[...] = jnp.zeros_like(c_scr)

        def blocks(start, strict):
            k = k_ref[pl.ds(start, blk), :]
            v = v_ref[pl.ds(start, blk), :]
            heads = range(2)
            zs = [_dot_nt(q_heads[hh], k) for hh in heads]
            sps = [_softplus2(z) for z in zs]
            if strict is not None:
                sps = [jnp.where(strict, sp, 0.0) for sp in sps]
            carries = [c_scr[hh] for hh in heads]
            parts = [[None] * nblk for _ in heads]
            for b in reversed(range(nblk)):
                cols = slice(b * cb, (b + 1) * cb)
                laters = []
                for hh in heads:
                    hi, lo = _split2(sps[hh][:, cols])
                    laters.append(_dot(jnp.concatenate([hi, lo], axis=1), incl))
                for hh in heads:
                    parts[hh][b] = jnp.exp2(zs[hh][:, cols] - laters[hh] - carries[hh])
                    carries[hh] = carries[hh] + jnp.sum(sps[hh][:, cols], axis=1, keepdims=True)
            wgts = [p[0] if nblk == 1 else jnp.concatenate(p, axis=1) for p in parts]
            if strict is not None:
                wgts = [jnp.where(strict, wgt, 0.0) for wgt in wgts]
            outs = [_dot(wgt.astype(BF16), v) for wgt in wgts]
            for hh in heads:
                acc_scr[hh] += outs[hh]
                c_scr[hh] = carries[hh]

        blocks(pl.multiple_of(qi * blk, blk), _iota((blk, blk), 1) < _iota((blk, blk), 0))

        def step(j, _):
            blocks(pl.multiple_of((qi - 1 - j) * blk, blk), None)
            return 0

        lax.fori_loop(0, qi, step, 0)
        o_ref[...] = jnp.where(lane < HEAD_DIM, acc_scr[0], acc_scr[1]).astype(o_ref.dtype)

    return pl.pallas_call(
        body, grid=(n_pairs, nq),
        in_specs=[pl.BlockSpec((blk, LANES), lambda p, i: (i, p)),
                  pl.BlockSpec((t, LANES), lambda p, i: (0, p)),
                  pl.BlockSpec((t, LANES), lambda p, i: (0, p))],
        out_specs=pl.BlockSpec((blk, LANES), lambda p, i: (i, p)),
        out_shape=jax.ShapeDtypeStruct((t, w), BF16),
        scratch_shapes=[pltpu.VMEM((2, blk, LANES), F32), pltpu.VMEM((2, blk, 1), F32)],
        compiler_params=_cparams("parallel", "arbitrary"), name="sb_prompt")(qb, kb, vb)


def _fox_prompt(qb, kb, vb, cum_row):
    t, w = qb.shape
    blk = _attn_block(t)
    nq = t // blk
    n_pairs = w // LANES

    def body(q_ref, k_ref, v_ref, ck_ref, o_ref, acc_scr, m_scr):
        p = pl.program_id(0)
        qi = pl.program_id(1)
        q = q_ref[...]
        lane = _iota((blk, LANES), 1)
        first = lane < HEAD_DIM
        zero = jnp.zeros_like(q)
        q_heads = (jnp.where(first, q, zero), jnp.where(first, zero, q))
        acc_scr[...] = jnp.zeros_like(acc_scr)
        m_scr[...] = jnp.full_like(m_scr, -jnp.inf)
        rsel = _iota((N_HEADS, blk), 0)
        heads = range(2)

        def blocks(start, causal):
            k = k_ref[pl.ds(start, blk), :]
            v = v_ref[pl.ds(start, blk), :]
            ck_all = ck_ref[:, pl.ds(start, blk)]
            one = jnp.ones_like(v)
            v_aug = (jnp.where(first, v, one), jnp.where(first, one, v))
            zs = [_dot_nt(q_heads[hh], k) for hh in heads]
            ss = []
            for hh in heads:
                ck = jnp.sum(jnp.where(rsel == 2 * p + hh, ck_all, 0.0), axis=0, keepdims=True)
                s = zs[hh] - ck
                ss.append(s if causal is None else jnp.where(causal, s, -jnp.inf))
            maxes = [jnp.max(s, axis=1, keepdims=True) for s in ss]
            m_olds = [m_scr[hh] for hh in heads]
            m_news = [jnp.maximum(m_olds[hh], maxes[hh]) for hh in heads]
            alphas = [jnp.exp2(m_olds[hh] - m_news[hh]) for hh in heads]
            prs = [jnp.exp2(ss[hh] - m_news[hh]).astype(BF16) for hh in heads]
            outs = [_dot(prs[hh], v_aug[hh]) for hh in heads]
            for hh in heads:
                acc_scr[hh] = alphas[hh] * acc_scr[hh] + outs[hh]
                m_scr[hh] = m_news[hh]

        blocks(pl.multiple_of(qi * blk, blk), _iota((blk, blk), 1) <= _iota((blk, blk), 0))

        def step(j, _):
            blocks(pl.multiple_of((qi - 1 - j) * blk, blk), None)
            return 0

        lax.fori_loop(0, qi, step, 0)
        outs = [acc_scr[hh] / pltpu.roll(acc_scr[hh], HEAD_DIM, 1) for hh in heads]
        o_ref[...] = jnp.where(first, outs[0], outs[1]).astype(o_ref.dtype)

    return pl.pallas_call(
        body, grid=(n_pairs, nq),
        in_specs=[pl.BlockSpec((blk, LANES), lambda p, i: (i, p)),
                  pl.BlockSpec((t, LANES), lambda p, i: (0, p)),
                  pl.BlockSpec((t, LANES), lambda p, i: (0, p)),
                  pl.BlockSpec((N_HEADS, t), lambda p, i: (0, 0))],
        out_specs=pl.BlockSpec((blk, LANES), lambda p, i: (i, p)),
        out_shape=jax.ShapeDtypeStruct((t, w), BF16),
        scratch_shapes=[pltpu.VMEM((2, blk, LANES), F32), pltpu.VMEM((2, blk, 1), F32)],
        compiler_params=_cparams("parallel", "arbitrary"), name="fox_prompt")(qb, kb, vb, cum_row)


def _gate_math(x, bias, a_log, idx):
    xb = x + bias
    e = jnp.exp(-jnp.abs(xb))
    l1p = jnp.log(1.0 + e)
    logf = jnp.minimum(xb, 0.0) - l1p
    g = -jnp.exp(a_log) * (jnp.maximum(xb, 0.0) + l1p)
    beta = 1.0 / (1.0 + jnp.exp(-xb))
    h = N_HEADS
    return jnp.where(idx < h, logf, jnp.where(idx < 2 * h, g, jnp.where(idx < 3 * h, beta, 0.0)))


def _gates(small_col, small_row, bias_lane, alog_lane, bias_col, alog_col, *, tm=2048):
    m = small_col.shape[0]
    r = small_row.shape[0]
    tm = min(tm, m)

    def body(c_ref, r_ref, bl_ref, al_ref, bc_ref, ac_ref, oc_ref, or_ref):
        oc_ref[...] = _gate_math(c_ref[...], bl_ref[...], al_ref[...], _iota((tm, LANES), 1))
        or_ref[...] = _gate_math(r_ref[...], bc_ref[...], ac_ref[...], _iota((r, tm), 0))

    return pl.pallas_call(
        body, grid=(m // tm,),
        in_specs=[pl.BlockSpec((tm, LANES), lambda i: (i, 0)), pl.BlockSpec((r, tm), lambda i: (0, i)),
                  pl.BlockSpec((1, LANES), lambda i: (0, 0)), pl.BlockSpec((1, LANES), lambda i: (0, 0)),
                  pl.BlockSpec((r, 1), lambda i: (0, 0)), pl.BlockSpec((r, 1), lambda i: (0, 0))],
        out_specs=[pl.BlockSpec((tm, LANES), lambda i: (i, 0)), pl.BlockSpec((r, tm), lambda i: (0, i))],
        out_shape=[jax.ShapeDtypeStruct((m, LANES), F32), jax.ShapeDtypeStruct((r, m), F32)],
        compiler_params=_cparams("parallel"), name="gates")(
            small_col, small_row, bias_lane, alog_lane, bias_col, alog_col)


def _cum_forget(logf_rows, n_chunks):
    r = logf_rows.shape[0]

    def body(x_ref, o_ref):
        x = x_ref[...]
        incl = jnp.where(_iota((LANES, LANES), 0) <= _iota((LANES, LANES), 1), 1.0, 0.0).astype(BF16)
        local = _dot_x3(x, incl)
        tot = jnp.broadcast_to(local[:, LANES - 1:LANES], (r, LANES))
        ri = _iota((r, r), 0)
        ci = _iota((r, r), 1)
        before = jnp.where((ri // n_chunks == ci // n_chunks) & (ci < ri), 1.0, 0.0).astype(BF16)
        o_ref[...] = (local + _dot_3x(before, tot)) * LOG2E

    return pl.pallas_call(
        body, grid=(1,),
        in_specs=[pl.BlockSpec((r, LANES), lambda i: (0, 0))],
        out_specs=pl.BlockSpec((r, LANES), lambda i: (0, 0)),
        out_shape=jax.ShapeDtypeStruct((r, LANES), F32),
        compiler_params=_cparams("arbitrary"), name="cum_forget")(logf_rows)


def _gdn_prep(dqkv, conv_w, *, tm=256):
    t, w3 = dqkv.shape
    w = w3 // 3
    tm = min(tm, t)
    scale = HEAD_DIM ** -0.5

    def body(x_ref, prev_ref, cw_ref, q_ref, k_ref, kt_ref, v_ref):
        i = pl.program_id(0)
        x = x_ref[...]
        prev = jnp.where(i == 0, 0.0, prev_ref[...])
        xp = jnp.concatenate([prev, x], axis=0)
        cw = cw_ref[...]
        off = SUBLANES - (CONV_W - 1)
        y = xp[off:off + tm] * cw[0:1]
        for j in range(1, CONV_W):
            y = y + xp[off + j:off + j + tm] * cw[j:j + 1]
        y = _silu(y)
        ones = _head_block_ones(w)

        def l2n(a):
            hi, lo = _split2(a * a)
            return a * lax.rsqrt(_dot(hi, ones) + _dot(lo, ones) + EPS)

        qn = l2n(y[:, :w]) * scale
        kn = l2n(y[:, w:2 * w])
        vv = y[:, 2 * w:]
        for h in range(N_HEADS):
            sl = slice(h * HEAD_DIM, (h + 1) * HEAD_DIM)
            q_ref[h] = qn[:, sl]
            k_ref[h] = kn[:, sl]
            v_ref[h] = vv[:, sl]
        kt_ref[...] = kn.T

    hm = pl.BlockSpec((N_HEADS, tm, HEAD_DIM), lambda i: (0, i, 0))
    hm_shape = jax.ShapeDtypeStruct((N_HEADS, t, HEAD_DIM), F32)
    return pl.pallas_call(
        body, grid=(t // tm,),
        in_specs=[pl.BlockSpec((tm, w3), lambda i: (i, 0)),
                  pl.BlockSpec((SUBLANES, w3), lambda i: (jnp.maximum(i * (tm // SUBLANES) - 1, 0), 0)),
                  pl.BlockSpec((CONV_W, w3), lambda i: (0, 0))],
        out_specs=[hm, hm, pl.BlockSpec((w, tm), lambda i: (0, i)), hm],
        out_shape=[hm_shape, hm_shape, jax.ShapeDtypeStruct((w, t), F32), hm_shape],
        compiler_params=_cparams("parallel"), name="gdn_prep")(dqkv, dqkv, conv_w)


def _mm3(a, b):
    ah, al = _split2(a)
    bh, bl = _split2(b)
    return _dot(ah, bh) + _dot(ah, bl) + _dot(al, bh)


def _gdn_wy(q, k, kt, v, post_col, post_row):
    nh, t, d = q.shape
    c = math.gcd(t, GDN_CHUNK)
    span = LANES if t % LANES == 0 else t
    per = span // c

    def body(q_ref, k_ref, kt_ref, v_ref, pc_ref, pr_ref, u_ref, w_ref, qe_ref, qk_ref, kdt_ref, egl_ref):
        ri = _iota((c, c), 0)
        ci = _iota((c, c), 1)
        tril = ri >= ci
        strict = ri > ci
        lower_incl = jnp.where(tril, 1.0, 0.0).astype(BF16)
        upper_incl = jnp.where(ri <= ci, 1.0, 0.0).astype(BF16)
        xs, ys = [], []
        units = [(s, h) for s in range(per) for h in range(nh)]
        cums = []
        for s in range(per):
            rows = slice(s * c, (s + 1) * c)
            cums.append((_dot_3x(lower_incl, pc_ref[rows, :]),
                         _dot_x3(pr_ref[:, rows], upper_incl)))
        for s, h in units:
            rows = slice(s * c, (s + 1) * c)
            cg_col_all, cg_row_all = cums[s]
            cgc = cg_col_all[:, nh + h:nh + h + 1]
            cgr = cg_row_all[nh + h:nh + h + 1, :]
            beta = pc_ref[rows, 2 * nh + h:2 * nh + h + 1]
            g_last = cgr[:, c - 1:c]
            decay = jnp.where(tril, jnp.exp(jnp.where(tril, cgc - cgr, 0.0)), 0.0)
            qh = q_ref[h, rows, :]
            kth = kt_ref[h * d:(h + 1) * d, rows]
            k_beta = k_ref[h, rows, :] * beta
            gram = _dot(jnp.concatenate([k_beta, qh], axis=0).astype(BF16), kth.astype(BF16))
            qk_ref[h, rows, :] = jnp.where(tril, gram[c:] * decay, 0.0).astype(BF16)
            qe_ref[h, rows, :] = (qh * jnp.exp(cgc)).astype(BF16)
            kdt_ref[h * d:(h + 1) * d, rows] = (kth * jnp.exp(g_last - cgr)).astype(BF16)
            egl_ref[h, rows, :] = jnp.broadcast_to(jnp.exp(g_last), (c, d))
            xs.append(jnp.where(strict, gram[:c] * decay, 0.0))
            ys.append(jnp.concatenate([v_ref[h, rows, :] * beta, k_beta * jnp.exp(cgc)], axis=1))
        base = GDN_INV_BASE
        assert c % base == 0 and (c // base) & (c // base - 1) == 0
        eye = jnp.where(ri == ci, 1.0, 0.0)
        n_u = len(units)

        def same_block(size):
            return (ri // size) == (ci // size)

        ts = [eye - jnp.where(same_block(base), a, 0.0) for a in xs]
        pws = [t - eye for t in ts]
        span_done = 2
        while span_done < base:
            sq = [_mm3(pws[i], pws[i]) for i in range(n_u)]
            ts = [ts[i] + _mm3(ts[i], sq[i]) for i in range(n_u)]
            pws = sq
            span_done *= 2
        size = base
        while 2 * size < c:
            coup = same_block(2 * size) & ~same_block(size)
            prods = [_mm3(jnp.where(coup, xs[i], 0.0), ts[i]) for i in range(n_u)]
            ts = [ts[i] - _mm3(ts[i], prods[i]) for i in range(n_u)]
            size *= 2
        if size < c:
            coup = ~same_block(size)
            tys = [_mm3(ts[i], ys[i]) for i in range(n_u)]
            prods = [_mm3(jnp.where(coup, xs[i], 0.0), tys[i]) for i in range(n_u)]
            ys = [tys[i] - _mm3(ts[i], prods[i]) for i in range(n_u)]
        else:
            ys = [_mm3(ts[i], ys[i]) for i in range(n_u)]
        for i, (s, h) in enumerate(units):
            rows = slice(s * c, (s + 1) * c)
            u_ref[h, rows, :] = ys[i][:, :d]
            w_ref[h, rows, :] = ys[i][:, d:].astype(BF16)

    hm = pl.BlockSpec((nh, span, d), lambda i: (0, i, 0))
    ktb = pl.BlockSpec((nh * d, span), lambda i: (0, i))
    return pl.pallas_call(
        body, grid=(t // span,),
        in_specs=[hm, hm, ktb, hm, pl.BlockSpec((span, LANES), lambda i: (i, 0)),
                  pl.BlockSpec((post_row.shape[0], span), lambda i: (0, i))],
        out_specs=[hm, hm, hm, hm, ktb, hm],
        out_shape=[jax.ShapeDtypeStruct((nh, t, d), F32), jax.ShapeDtypeStruct((nh, t, d), BF16),
                   jax.ShapeDtypeStruct((nh, t, d), BF16), jax.ShapeDtypeStruct((nh, t, d), BF16),
                   jax.ShapeDtypeStruct((nh * d, t), BF16), jax.ShapeDtypeStruct((nh, t, d), F32)],
        compiler_params=_cparams("parallel"), name="gdn_wy")(q, k, kt, v, post_col, post_row)


def _gdn_scan(u, w, qe, qk, kdt, egl, dz, g_gdn):
    nh, t, d = u.shape
    c = math.gcd(t, GDN_CHUNK)
    span = LANES if t % LANES == 0 else t
    per = span // c
    n = t // span

    def body(u_ref, w_ref, qe_ref, qk_ref, kdt_ref, egl_ref, dz_ref, gg_ref, o_ref, s_out_ref, s_scr, o_scr):
        i = pl.program_id(0)

        @pl.when(i == 0)
        def _():
            s_scr[...] = jnp.zeros_like(s_scr)

        heads = range(nh)
        states = [s_scr[h] for h in heads]
        for j in range(per):
            rows = slice(j * c, (j + 1) * c)
            sbs = [states[h].astype(BF16) for h in heads]
            ws = [_dot(w_ref[h, rows, :], sbs[h]) for h in heads]
            qs = [_dot(qe_ref[h, rows, :], sbs[h]) for h in heads]
            vbs = [(u_ref[h, rows, :] - ws[h]).astype(BF16) for h in heads]
            os_ = [qs[h] + _dot(qk_ref[h, rows, :], vbs[h]) for h in heads]
            upd = [_dot(kdt_ref[h * d:(h + 1) * d, rows], vbs[h]) for h in heads]
            for h in heads:
                states[h] = states[h] * egl_ref[h, rows, :] + upd[h]
                o = os_[h]
                y = o * lax.rsqrt(jnp.mean(o * o, axis=-1, keepdims=True) + EPS) * gg_ref[...]
                o_scr[rows, h * d:(h + 1) * d] = y
        for h in heads:
            s_scr[h] = states[h]
        o_ref[...] = (o_scr[...] * _silu(dz_ref[...])).astype(o_ref.dtype)

        @pl.when(i == n - 1)
        def _():
            s_out_ref[...] = s_scr[...]

    hm = pl.BlockSpec((nh, span, d), lambda i: (0, i, 0))
    return pl.pallas_call(
        body, grid=(n,),
        in_specs=[hm, hm, hm, hm, pl.BlockSpec((nh * d, span), lambda i: (0, i)), hm,
                  pl.BlockSpec((span, nh * d), lambda i: (i, 0)), pl.BlockSpec((1, d), lambda i: (0, 0))],
        out_specs=[pl.BlockSpec((span, nh * d), lambda i: (i, 0)),
                   pl.BlockSpec((nh, d, d), lambda i: (0, 0, 0))],
        out_shape=[jax.ShapeDtypeStruct((t, nh * d), BF16), jax.ShapeDtypeStruct((nh, d, d), F32)],
        scratch_shapes=[pltpu.VMEM((nh, d, d), F32), pltpu.VMEM((span, nh * d), F32)],
        compiler_params=_cparams("arbitrary"), name="gdn_scan")(u, w, qe, qk, kdt, egl, dz, g_gdn)


def _mem_attn_core(q, mk, mv, precise):
    scale = MEM_HEAD_DIM ** -0.5
    outs = []
    for h in range(MEM_HEADS):
        sl = slice(h * MEM_HEAD_DIM, (h + 1) * MEM_HEAD_DIM)
        if precise:
            s = _dot_nt_hi(q[:, sl], mk[:, sl]) * scale
        else:
            s = _dot_nt(q[:, sl].astype(BF16), mk[:, sl]) * scale
        e = jnp.exp(s - jnp.max(s, axis=-1, keepdims=True))
        p = e / jnp.sum(e, axis=-1, keepdims=True)
        outs.append(_dot_hi(p, mv[:, sl]) if precise else _dot(p.astype(BF16), mv[:, sl]))
    return jnp.concatenate(outs, axis=-1)


def _mem_attn_prompt(y, g, wq, mk, mv, wo, *, tm=512):
    m, d = y.shape
    tm = min(tm, m)

    def body(y_ref, g_ref, wq_ref, mk_ref, mv_ref, wo_ref, o_ref):
        yv = y_ref[...]
        hb = _rms(yv, g_ref[...]).astype(BF16)
        q = _dot(hb, wq_ref[...])
        o = _mem_attn_core(q, mk_ref[...], mv_ref[...], False)
        o_ref[...] = yv + _dot(o.astype(BF16), wo_ref[...])

    full = lambda a: pl.BlockSpec(a.shape, lambda i: (0, 0))
    return pl.pallas_call(
        body, grid=(m // tm,),
        in_specs=[pl.BlockSpec((tm, d), lambda i: (i, 0)), pl.BlockSpec((1, d), lambda i: (0, 0)),
                  full(wq), full(mk), full(mv), full(wo)],
        out_specs=pl.BlockSpec((tm, d), lambda i: (i, 0)),
        out_shape=jax.ShapeDtypeStruct((m, d), F32),
        compiler_params=_cparams("parallel"), name="mem_attn_prompt")(y, g, wq, mk, mv, wo)


def _mem_attn_sample(q, mem_k, mem_v):
    b, _, w = q.shape
    n_mem = mem_k.shape[1]

    def body(q_ref, k_ref, v_ref, o_ref):
        q8 = jnp.broadcast_to(q_ref[0], (SUBLANES, w))
        o = _mem_attn_core(q8, k_ref[0], v_ref[0], True)
        o_ref[0] = o[0:1]

    return pl.pallas_call(
        body, grid=(b,),
        in_specs=[pl.BlockSpec((1, 1, w), lambda i: (i, 0, 0)),
                  pl.BlockSpec((1, n_mem, w), lambda i: (i, 0, 0)),
                  pl.BlockSpec((1, n_mem, w), lambda i: (i, 0, 0))],
        out_specs=pl.BlockSpec((1, 1, w), lambda i: (i, 0, 0)),
        out_shape=jax.ShapeDtypeStruct((b, 1, w), F32),
        compiler_params=_cparams("parallel"), name="mem_attn_sample")(q, mem_k, mem_v)


def _route(logits):
    m = logits.shape[0]
    lane = _iota((m, LANES), 1)
    big = LANES
    neg = -jnp.inf
    is_g = lane < N_GROUPS
    gl = jnp.where(is_g, logits, neg)
    gmax = jnp.max(gl, axis=1, keepdims=True)
    g_val = 1.0 / jnp.sum(jnp.where(is_g, jnp.exp(gl - gmax), 0.0), axis=1, keepdims=True)
    g_idx = jnp.min(jnp.where(is_g & (gl == gmax), lane, big), axis=1, keepdims=True)
    lo = N_GROUPS + g_idx * EXPERTS_PER_GROUP
    in_grp = (lane >= lo) & (lane < lo + EXPERTS_PER_GROUP)
    el = jnp.where(in_grp, logits, neg)
    l1 = jnp.max(el, axis=1, keepdims=True)
    i1 = jnp.min(jnp.where(in_grp & (el == l1), lane, big), axis=1, keepdims=True)
    el2 = jnp.where(lane == i1, neg, el)
    l2 = jnp.max(el2, axis=1, keepdims=True)
    i2 = jnp.min(jnp.where(in_grp & (lane != i1) & (el2 == l2), lane, big), axis=1, keepdims=True)
    p2 = jnp.exp(l2 - l1)
    w1 = g_val / (1.0 + p2)
    w2 = g_val * p2 / (1.0 + p2)
    return jnp.where(lane == i1, w1, jnp.where(lane == i2, w2, 0.0))


def _moe_dense(y, g, w_route, w_gate, w_up, w_down, *, precise=False, tm=1024, final_g=None):
    m, d = y.shape
    tm = min(tm, m)
    ne, _, ff = w_gate.shape
    xdt = F32 if precise else BF16

    def body(*refs):
        y_ref, g_ref, wr_ref, wg_ref, wu_ref, wd_ref = refs[:6]
        pos = 6
        fg_ref = None
        if final_g is not None:
            fg_ref = refs[6]
            pos = 7
        o_ref = refs[pos]
        pos += 1
        on_ref = None
        if final_g is not None:
            on_ref = refs[pos]
            pos += 1
        xn_scr, gate_scr, acc_scr = refs[pos:pos + 3]
        e = pl.program_id(1)

        @pl.when(e == 0)
        def _():
            xn = _rms(y_ref[...], g_ref[...])
            xn_scr[...] = xn.astype(xdt)
            logits = _dot_hi(xn, wr_ref[...]) if precise else _dot(xn.astype(BF16), wr_ref[...])
            gate_scr[...] = _route(logits)
            acc_scr[...] = jnp.zeros_like(acc_scr)

        xn = xn_scr[...]
        if precise:
            hg = _dot_hi(xn, wg_ref[0])
            hu = _dot_hi(xn, wu_ref[0])
            out = _dot_hi(_silu(hg) * hu, wd_ref[0])
        else:
            hg = _dot(xn, wg_ref[0])
            hu = _dot(xn, wu_ref[0])
            out = _dot((_silu(hg) * hu).astype(BF16), wd_ref[0])
        lane = _iota((tm, LANES), 1)
        ge = jnp.sum(jnp.where(lane == N_GROUPS + e, gate_scr[...], 0.0), axis=1, keepdims=True)
        acc_scr[...] += ge * out

        @pl.when(e == ne - 1)
        def _():
            res = y_ref[...] + acc_scr[...]
            o_ref[...] = res
            if on_ref is not None:
                on_ref[...] = _rms(res, fg_ref[...])

    tok = pl.BlockSpec((tm, d), lambda i, e: (i, 0))
    in_specs = [tok, pl.BlockSpec((1, d), lambda i, e: (0, 0)), pl.BlockSpec(w_route.shape, lambda i, e: (0, 0)),
                pl.BlockSpec((1, d, ff), lambda i, e: (e, 0, 0)), pl.BlockSpec((1, d, ff), lambda i, e: (e, 0, 0)),
                pl.BlockSpec((1, ff, d), lambda i, e: (e, 0, 0))]
    args = [y, g, w_route, w_gate, w_up, w_down]
    out_shape = [jax.ShapeDtypeStruct((m, d), F32)]
    out_specs = [tok]
    if final_g is not None:
        in_specs.append(pl.BlockSpec((1, d), lambda i, e: (0, 0)))
        args.append(final_g)
        out_shape.append(jax.ShapeDtypeStruct((m, d), F32))
        out_specs.append(tok)
    res = pl.pallas_call(
        body, grid=(m // tm, ne), in_specs=in_specs, out_specs=out_specs, out_shape=out_shape,
        scratch_shapes=[pltpu.VMEM((tm, d), xdt), pltpu.VMEM((tm, LANES), F32), pltpu.VMEM((tm, d), F32)],
        compiler_params=_cparams("parallel", "arbitrary"), name="moe_dense")(*args)
    return res[0] if final_g is None else res


def _heads3(x):
    return x.reshape(N_HEADS, HEAD_DIM, x.shape[-1])


def _retention_sample(q_col, k_col, v, gate, state, g_ret, inv_col, pos):
    b, w, _ = q_col.shape
    scale = HEAD_DIM ** -0.5
    half = HEAD_DIM // 2

    def body(q_ref, k_ref, v_ref, gate_ref, s_ref, gr_ref, inv_ref, o_ref, so_ref):
        row = _iota((w, 1), 0)
        first_half = (row % HEAD_DIM) < half
        ang = float(pos) * inv_ref[...]
        cos, sin = jnp.cos(ang), jnp.sin(ang)

        def rot(x):
            return x * cos + jnp.where(first_half, -pltpu.roll(x, w - half, 0), pltpu.roll(x, half, 0)) * sin

        q = rot(q_ref[0])
        k = rot(k_ref[0]) * scale
        gamma = 1.0 - jnp.exp2(-5.0 - (row // HEAD_DIM).astype(F32))
        s = s_ref[0]
        vv = v_ref[0]
        qk = jnp.sum(_heads3(q * k), axis=1)
        qs = jnp.sum(_heads3(q * s * gamma), axis=1)
        o = qk * vv + qs
        v_rows = jnp.broadcast_to(vv[:, None, :], (N_HEADS, HEAD_DIM, HEAD_DIM)).reshape(w, HEAD_DIM)
        so_ref[0] = s * gamma + k * v_rows
        mu = jnp.mean(o, axis=-1, keepdims=True)
        dlt = o - mu
        var = jnp.mean(dlt * dlt, axis=-1, keepdims=True)
        o_ref[0] = dlt * lax.rsqrt(var + EPS) * gr_ref[...] * _silu(gate_ref[0])

    col = pl.BlockSpec((1, w, 1), lambda i: (i, 0, 0))
    hd = pl.BlockSpec((1, N_HEADS, HEAD_DIM), lambda i: (i, 0, 0))
    st = pl.BlockSpec((1, w, HEAD_DIM), lambda i: (i, 0, 0))
    return pl.pallas_call(
        body, grid=(b,),
        in_specs=[col, col, hd, hd, st, pl.BlockSpec((N_HEADS, HEAD_DIM), lambda i: (0, 0)),
                  pl.BlockSpec((w, 1), lambda i: (0, 0))],
        out_specs=[hd, st],
        out_shape=[jax.ShapeDtypeStruct((b, N_HEADS, HEAD_DIM), F32), jax.ShapeDtypeStruct((b, w, HEAD_DIM), F32)],
        compiler_params=_cparams("parallel"), name="retention_sample")(q_col, k_col, v, gate, state, g_ret, inv_col)


def _gdn_sample(xqk_col, bufqk_col, cwqk_col, xv, bufv, cwv, dz, small_col, bias_col, alog_col, state, g_gdn):
    b = xqk_col.shape[0]
    w = N_HEADS * HEAD_DIM
    scale = HEAD_DIM ** -0.5
    r = small_col.shape[1]

    def body(x_ref, bq_ref, cq_ref, xv_ref, bv_ref, cv_ref, dz_ref, sm_ref, bc_ref, ac_ref, s_ref, gg_ref,
             o_ref, so_ref):
        yqk = x_ref[0] * cq_ref[CONV_W - 1]
        yv = xv_ref[0] * cv_ref[CONV_W - 1]
        for j in range(CONV_W - 1):
            yqk = yqk + bq_ref[0, j] * cq_ref[j]
            yv = yv + bv_ref[0, j] * cv_ref[j]
        yqk = _silu(yqk)
        vv = _silu(yv)

        def l2n(a):
            a3 = _heads3(a)
            ss = jnp.sum(a3 * a3, axis=1, keepdims=True)
            return a3 * lax.rsqrt(ss + EPS)

        q3 = l2n(yqk[:w]) * scale
        k3 = l2n(yqk[w:])
        post = _gate_math(sm_ref[0], bc_ref[...], ac_ref[...], _iota((r, 1), 0))
        g = post[N_HEADS:2 * N_HEADS]
        beta = post[2 * N_HEADS:3 * N_HEADS]
        a = jnp.exp(g)
        s3 = _heads3(s_ref[0])
        ks = jnp.sum(k3 * s3, axis=1)
        v_new = beta * vv - (beta * a) * ks
        o = a * jnp.sum(q3 * s3, axis=1) + jnp.sum(q3 * k3, axis=1) * v_new
        s_new = s3 * a[:, :, None] + k3 * v_new[:, None, :]
        so_ref[0] = s_new.reshape(w, HEAD_DIM)
        y = o * lax.rsqrt(jnp.mean(o * o, axis=-1, keepdims=True) + EPS) * gg_ref[...]
        o_ref[0] = y * _silu(dz_ref[0])

    hd = pl.BlockSpec((1, N_HEADS, HEAD_DIM), lambda i: (i, 0, 0))
    st = pl.BlockSpec((1, w, HEAD_DIM), lambda i: (i, 0, 0))
    return pl.pallas_call(
        body, grid=(b,),
        in_specs=[pl.BlockSpec((1, 2 * w, 1), lambda i: (i, 0, 0)),
                  pl.BlockSpec((1, CONV_W - 1, 2 * w, 1), lambda i: (i, 0, 0, 0)),
                  pl.BlockSpec((CONV_W, 2 * w, 1), lambda i: (0, 0, 0)),
                  hd, pl.BlockSpec((1, CONV_W - 1, N_HEADS, HEAD_DIM), lambda i: (i, 0, 0, 0)),
                  pl.BlockSpec((CONV_W, N_HEADS, HEAD_DIM), lambda i: (0, 0, 0)),
                  hd, pl.BlockSpec((1, r, 1), lambda i: (i, 0, 0)),
                  pl.BlockSpec((r, 1), lambda i: (0, 0)), pl.BlockSpec((r, 1), lambda i: (0, 0)),
                  st, pl.BlockSpec((1, HEAD_DIM), lambda i: (0, 0))],
        out_specs=[hd, st],
        out_shape=[jax.ShapeDtypeStruct((b, N_HEADS, HEAD_DIM), F32), jax.ShapeDtypeStruct((b, w, HEAD_DIM), F32)],
        compiler_params=_cparams("parallel"), name="gdn_sample")(
            xqk_col, bufqk_col, cwqk_col, xv, bufv, cwv, dz, small_col, bias_col, alog_col, state, g_gdn)


PAGES_PER_STEP = 8


def _paged_attend(mode, q_hd, cache_k, cache_v, page_table, page_base, *, k_new=None, v_new=None,
                  cache_logf=None, logf_new=None):
    fox = mode == "fox"
    b, nh, hd = q_hd.shape
    n_pages = page_table.shape[1]
    page = cache_k.shape[1]
    flat = page * nh
    assert flat % LANES == 0 and LANES % nh == 0
    nr = flat // LANES
    pps = math.gcd(n_pages, PAGES_PER_STEP)
    ng = n_pages // pps
    qscale = hd ** -0.5 * LOG2E
    n_split = 3 if fox else 2

    def body(*refs):
        q_ref = refs[1]
        pos = 2
        k_refs = refs[pos:pos + pps]
        pos += pps
        v_refs = refs[pos:pos + pps]
        pos += pps
        if fox:
            lf_refs = refs[pos:pos + pps]
            pos += pps
            kn_ref, vn_ref, lfn_ref = refs[pos:pos + 3]
            pos += 3
        o_ref = refs[pos]
        pos += 1
        acc_scr, carry_scr = refs[pos:pos + 2]
        pos += 2
        if fox:
            m_scr, l_scr = refs[pos:pos + 2]
        j = pl.program_id(1)
        hmask = (_iota((nh, LANES), 1) % nh) == _iota((nh, LANES), 0)
        hsel = _iota((nh, LANES), 1) == _iota((nh, LANES), 0)
        row_r = _iota((nr, LANES), 0)
        li = _iota((LANES, LANES), 0)
        lj = _iota((LANES, LANES), 1)
        same = (li % nh) == (lj % nh)
        later = same & ((li > lj) if fox else (li >= lj))
        mats = jnp.concatenate([jnp.where(later, 1.0, 0.0), jnp.where(same, 1.0, 0.0)], axis=1).astype(BF16)
        mats = jnp.concatenate([mats] * n_split, axis=0)

        def to_dense(zt):
            rows = [jnp.sum(jnp.where(hmask, zt[:, r * LANES:(r + 1) * LANES], 0.0), axis=0, keepdims=True)
                    for r in range(nr)]
            return jnp.concatenate(rows, axis=0)

        def from_dense(w8):
            cols = [jnp.where(hmask, jnp.broadcast_to(w8[r:r + 1, :], (nh, LANES)), 0.0) for r in range(nr)]
            return jnp.concatenate(cols, axis=1).astype(BF16)

        def rows_after(x):
            out = jnp.zeros_like(x)
            for kk in range(1, nr):
                out = out + jnp.where(row_r + kk < nr, pltpu.roll(x, nr - kk, 0), 0.0)
            return out

        def over_head_lanes(x, op):
            sh = nh
            while sh < LANES:
                x = op(x, pltpu.roll(x, sh, 1))
                sh *= 2
            return x

        def to_lanes(col):
            return jnp.sum(jnp.where(hmask, jnp.broadcast_to(col, (nh, LANES)), 0.0), axis=0, keepdims=True)

        def to_col(lanes):
            return jnp.sum(jnp.where(hsel, jnp.broadcast_to(lanes, (nh, LANES)), 0.0), axis=1, keepdims=True)

        qb = (q_ref[0] * qscale).astype(BF16)

        @pl.when(j == 0)
        def _():
            if fox:
                s_own = jnp.sum(qb.astype(F32) * kn_ref[0].astype(BF16).astype(F32), axis=1, keepdims=True)
                m_scr[...] = to_lanes(s_own)
                l_scr[...] = jnp.ones_like(l_scr)
                acc_scr[...] = vn_ref[0]
                carry_scr[...] = lfn_ref[0] * LOG2E
            else:
                acc_scr[...] = jnp.zeros_like(acc_scr)
                carry_scr[...] = jnp.zeros_like(carry_scr)

        pages = range(pps)
        kbs = [k_refs[p][0].reshape(flat, hd).astype(BF16) for p in pages]
        z8s = [to_dense(_dot_nt(qb, kbs[p])) for p in pages]
        carry = carry_scr[...]
        if not fox:
            sps = [_softplus2(z8) for z8 in z8s]
            sums = [_dot(jnp.concatenate(_split2(sp), axis=1), mats) for sp in sps]
            wts = []
            for p in pages:
                within, rowtot = sums[p][:, :LANES], sums[p][:, LANES:]
                wts.append(jnp.exp2(z8s[p] - within - rows_after(rowtot) - carry))
                carry = carry + jnp.sum(rowtot, axis=0, keepdims=True)
            acc = acc_scr[...]
        else:
            lfs = [lf_refs[p][0] * LOG2E for p in pages]
            sums = [_dot(jnp.concatenate(_split3(lf), axis=1), mats) for lf in lfs]
            logits = []
            for p in pages:
                within, rowtot = sums[p][:, :LANES], sums[p][:, LANES:]
                logits.append(z8s[p] + (within + rows_after(rowtot) + carry))
                carry = carry + jnp.sum(rowtot, axis=0, keepdims=True)
            m_old = m_scr[...]
            m_new = m_old
            for s in logits:
                m_new = jnp.maximum(m_new, over_head_lanes(jnp.max(s, axis=0, keepdims=True), jnp.maximum))
            alpha = jnp.exp2(m_old - m_new)
            wts = [jnp.exp2(s - m_new) for s in logits]
            l_new = alpha * l_scr[...]
            for wt in wts:
                l_new = l_new + over_head_lanes(jnp.sum(wt, axis=0, keepdims=True), jnp.add)
            l_scr[...] = l_new
            m_scr[...] = m_new
            acc = acc_scr[...] * to_col(alpha)
        carry_scr[...] = carry
        outs = [_dot(from_dense(wts[p]), v_refs[p][0].reshape(flat, hd).astype(BF16)) for p in pages]
        for o in outs:
            acc = acc + o
        acc_scr[...] = acc

        @pl.when(j == ng - 1)
        def _():
            res = acc_scr[...]
            o_ref[0] = res / to_col(l_scr[...]) if fox else res

    def page_map(p):
        def index_map(i, j, pt):
            return (page_base + pt[i, (ng - 1 - j) * pps + (pps - 1 - p)], 0, 0, 0)
        return index_map

    def dense_map(p):
        def index_map(i, j, pt):
            return (page_base + pt[i, (ng - 1 - j) * pps + (pps - 1 - p)], 0, 0)
        return index_map

    per_seq = lambda i, j, pt: (i, 0, 0)
    hd_spec = pl.BlockSpec((1, nh, hd), per_seq)
    in_specs = [hd_spec]
    in_specs += [pl.BlockSpec((1, page, nh, hd), page_map(p)) for p in range(pps)]
    in_specs += [pl.BlockSpec((1, page, nh, hd), page_map(p)) for p in range(pps)]
    args = [q_hd] + [cache_k] * pps + [cache_v] * pps
    scratch = [pltpu.VMEM((nh, hd), F32), pltpu.VMEM((1, LANES), F32)]
    if fox:
        in_specs += [pl.BlockSpec((1, nr, LANES), dense_map(p)) for p in range(pps)]
        in_specs += [hd_spec, hd_spec, pl.BlockSpec((1, 1, LANES), per_seq)]
        args += [cache_logf] * pps + [k_new, v_new, logf_new]
        scratch += [pltpu.VMEM((1, LANES), F32), pltpu.VMEM((1, LANES), F32)]
    return pl.pallas_call(
        body,
        grid_spec=pltpu.PrefetchScalarGridSpec(
            num_scalar_prefetch=1, grid=(b, ng), in_specs=in_specs, out_specs=hd_spec, scratch_shapes=scratch),
        out_shape=jax.ShapeDtypeStruct((b, nh, hd), F32),
        compiler_params=_cparams("parallel", "arbitrary"), name="paged_attend_" + mode)(page_table, *args)


def kernel(x_prompt, x_sample, state_ret, cache_sb_k, cache_sb_v, cache_fox_k, cache_fox_v, cache_fox_logf, state_gdn, state_conv, cache_mem_k, cache_mem_v, page_table, mem_prompt, g_mix, ev_w_in, ev_g_ret, ev_w_out, od_w_in, od_b_forget, od_conv_w, od_a_log, od_dt_bias, od_g_gdn, od_w_out, g_xattn, g_mem, w_mq, w_mk, w_mv, w_mo, g_ffn, w_route_group, w_route_expert, w_exp_gate, w_exp_up, w_exp_down, g_final):
    bp, t, d = x_prompt.shape
    bs, ts, _ = x_sample.shape
    assert bp == 1 and ts == 1 and t >= CONV_W - 1
    depth = g_mix.shape[0]
    n_pages = page_table.shape[1]
    n_phys, page = cache_sb_k.shape[1], cache_sb_k.shape[2]
    past_len = n_pages * page
    w = MIX_HALF
    nh, hd = N_HEADS, HEAD_DIM
    attn_scale = hd ** -0.5 * LOG2E
    f32o = [(F32, 1.0)]
    both = [(F32, 1.0), (BF16, 1.0)]
    row2 = lambda v: v.reshape(1, -1)

    half = hd // 2
    inv = ROPE_BASE ** (-jnp.arange(half, dtype=F32) / half)
    inv_lane = jnp.tile(inv, LANES // half)[None]
    inv_col = jnp.tile(inv, w // half)[:, None]

    yp = x_prompt[0]
    ys = x_sample[:, 0]
    y_prompt = y_sample = None
    ret_p, sbk_p, sbv_p, foxk_p, foxv_p, foxl_p, gdn_p, conv_p, memk_p, memv_p = ([] for _ in range(10))
    ret_s, sbk_s, sbv_s, foxk_s, foxv_s, foxl_s, gdn_s, conv_s = ([] for _ in range(8))

    for layer in range(depth):
        i = layer // 2
        g_in = row2(g_mix[layer])
        if layer % 2 == 0:
            w_in = ev_w_in[i]
            w_out = ev_w_out[i]
            outs = [(0, w, f32o), (w, w, f32o), (2 * w, w, f32o), (3 * w, w, f32o),
                    (4 * w, w, [(BF16, attn_scale)]), (5 * w, w, both), (6 * w, w, both)]
            rq, rk, rv, rg, sqb, sk, skb, sv, svb = _norm_proj(yp, g_in, w_in.astype(BF16), outs)
            o_ret, s_ret = _retention_prompt(rq, rk, rv, rg, row2(ev_g_ret[i]), inv_lane)
            o_sb = _sb_prompt(sqb, skb, svb)
            w_out_b = w_out.astype(BF16)
            yp = _resid_proj(yp, [o_ret, o_sb], [w_out_b[:w], w_out_b[w:]])
            ret_p.append(s_ret[None])
            sbk_p.append(sk.reshape(1, t, nh, hd))
            sbv_p.append(sv.reshape(1, t, nh, hd))
            outs = [(c * w, w, f32o) for c in range(7)]
            rq, rk, rv, rg, sq, sk, sv = _norm_proj(ys, g_in, w_in, outs, precise=True)
            o_ret, s_new = _retention_sample(
                rq.reshape(bs, w, 1), rk.reshape(bs, w, 1), rv.reshape(bs, nh, hd), rg.reshape(bs, nh, hd),
                state_ret[i].reshape(bs, w, hd), ev_g_ret[i].reshape(nh, hd), inv_col, past_len)
            o_sb = _paged_attend("sb", sq.reshape(bs, nh, hd), cache_sb_k.reshape(-1, page, nh, hd),
                                 cache_sb_v.reshape(-1, page, nh, hd), page_table, i * n_phys)
            ys = _resid_proj(ys, [o_ret.reshape(bs, w), o_sb.reshape(bs, w)], [w_out[:w], w_out[w:]], precise=True)
            ret_s.append(s_new.reshape(bs, nh, hd, hd))
            sbk_s.append(sk.reshape(bs, 1, nh, hd))
            sbv_s.append(sv.reshape(bs, 1, nh, hd))
        else:
            w_full = od_w_in[i]
            w_out = od_w_out[i]
            c_fl = 3 * w
            c_qkv = c_fl + nh
            c_dz = c_qkv + 3 * w
            c_da = c_dz + w
            n_small = 3 * nh
            small_w = jnp.concatenate([w_full[:, c_fl:c_qkv], w_full[:, c_da:c_da + 2 * nh],
                                       jnp.zeros((d, LANES - n_small), F32)], axis=1)
            w_in = jnp.concatenate([w_full[:, :c_fl], w_full[:, c_qkv:c_da], small_w], axis=1)
            o_dqkv, o_dz, o_small = 3 * w, 6 * w, 7 * w
            zeros8 = jnp.zeros((nh,), F32)
            bias = jnp.concatenate([od_b_forget[i], od_dt_bias[i], zeros8])
            alog = jnp.concatenate([zeros8, od_a_log[i], zeros8])
            n_rows = 4 * nh
            bias_lane = jnp.pad(bias, (0, LANES - n_small))[None]
            alog_lane = jnp.pad(alog, (0, LANES - n_small))[None]
            bias_col = jnp.pad(bias, (0, n_rows - n_small))[:, None]
            alog_col = jnp.pad(alog, (0, n_rows - n_small))[:, None]
            outs = [(0, w, [(BF16, attn_scale)]), (w, w, both), (2 * w, w, both),
                    (o_dqkv, 3 * w, f32o), (o_dz, w, f32o), (o_small, LANES, f32o)]
            w_in_b = w_in.astype(BF16)
            small_wt = small_w[:, :n_rows].T.astype(BF16)
            fqb, fk, fkb, fv, fvb, dqkv, dz, small_col, small_row = _norm_proj(yp, g_in, w_in_b, outs, wt=small_wt)
            post_col, post_row = _gates(small_col, small_row, bias_lane, alog_lane, bias_col, alog_col)
            n_chunks = t // LANES
            cum_row = _cum_forget(post_row[:nh].reshape(nh * n_chunks, LANES), n_chunks).reshape(nh, t)
            o_fox = _fox_prompt(fqb, fkb, fvb, cum_row)
            gq, gk, gkt, gv = _gdn_prep(dqkv, od_conv_w[i])
            wy = _gdn_wy(gq, gk, gkt, gv, post_col, post_row)
            o_gdn, s_gdn = _gdn_scan(*wy, dz, row2(od_g_gdn[i]))
            w_out_b = w_out.astype(BF16)
            yp = _resid_proj(yp, [o_fox, o_gdn], [w_out_b[:w], w_out_b[w:]])
            foxk_p.append(fk.reshape(1, t, nh, hd))
            foxv_p.append(fv.reshape(1, t, nh, hd))
            foxl_p.append(post_col[:, :nh].reshape(1, t, nh))
            gdn_p.append(s_gdn[None])
            conv_p.append(dqkv[t - (CONV_W - 1):][None])
            outs = [(0, w, f32o), (w, w, f32o), (2 * w, w, f32o), (o_dqkv, 3 * w, f32o), (o_dz, w, f32o),
                    (o_small, LANES, f32o)]
            fq, fk, fv, dqkv, dz, small = _norm_proj(ys, g_in, w_in, outs, precise=True)
            post_s, _ = _gates(small, small[:, :n_rows].T, bias_lane, alog_lane, bias_col, alog_col)
            o_fox = _paged_attend(
                "fox", fq.reshape(bs, nh, hd), cache_fox_k.reshape(-1, page, nh, hd),
                cache_fox_v.reshape(-1, page, nh, hd), page_table, i * n_phys,
                k_new=fk.reshape(bs, nh, hd), v_new=fv.reshape(bs, nh, hd),
                cache_logf=cache_fox_logf.reshape(-1, page * nh // LANES, LANES),
                logf_new=jnp.tile(post_s[:, :nh], (1, LANES // nh)).reshape(bs, 1, LANES))
            buf = state_conv[i]
            cw = od_conv_w[i]
            o_gdn, s_new = _gdn_sample(
                dqkv[:, :2 * w].reshape(bs, 2 * w, 1), buf[:, :, :2 * w].reshape(bs, CONV_W - 1, 2 * w, 1),
                cw[:, :2 * w].reshape(CONV_W, 2 * w, 1), dqkv[:, 2 * w:].reshape(bs, nh, hd),
                buf[:, :, 2 * w:].reshape(bs, CONV_W - 1, nh, hd), cw[:, 2 * w:].reshape(CONV_W, nh, hd),
                dz.reshape(bs, nh, hd), small[:, :n_rows].reshape(bs, n_rows, 1), bias_col, alog_col,
                state_gdn[i].reshape(bs, w, hd), row2(od_g_gdn[i]))
            ys = _resid_proj(ys, [o_fox.reshape(bs, w), o_gdn.reshape(bs, w)], [w_out[:w], w_out[w:]], precise=True)
            foxk_s.append(fk.reshape(bs, 1, nh, hd))
            foxv_s.append(fv.reshape(bs, 1, nh, hd))
            foxl_s.append(post_s[:, :nh].reshape(bs, 1, nh))
            gdn_s.append(s_new.reshape(bs, nh, hd, hd))
            conv_s.append(jnp.concatenate([buf[:, 1:], dqkv[:, None]], axis=1))

        n_mem = mem_prompt.shape[1]
        mw = MEM_HEADS * MEM_HEAD_DIM
        w_kv = jnp.concatenate([w_mk[layer], w_mv[layer]], axis=1).astype(BF16)
        mk, mkb, mv, mvb = _norm_proj(mem_prompt[0], row2(g_mem[layer]), w_kv, [(0, mw, both), (mw, mw, both)])
        memk_p.append(mk.reshape(1, n_mem, MEM_HEADS, MEM_HEAD_DIM))
        memv_p.append(mv.reshape(1, n_mem, MEM_HEADS, MEM_HEAD_DIM))
        g_x = row2(g_xattn[layer])
        yp = _mem_attn_prompt(yp, g_x, w_mq[layer].astype(BF16), mkb, mvb, w_mo[layer].astype(BF16))
        (q_s,) = _norm_proj(ys, g_x, w_mq[layer], [(0, mw, f32o)], precise=True)
        o_mem = _mem_attn_sample(q_s.reshape(bs, 1, mw), cache_mem_k[layer].reshape(bs, n_mem, mw),
                                 cache_mem_v[layer].reshape(bs, n_mem, mw))
        ys = _resid_proj(ys, [o_mem.reshape(bs, mw)], [w_mo[layer]], precise=True)

        w_route = jnp.concatenate([w_route_group[layer], w_route_expert[layer],
                                   jnp.zeros((d, LANES - N_GROUPS - N_EXPERTS), F32)], axis=1)
        g_f = row2(g_ffn[layer])
        fin = row2(g_final) if layer == depth - 1 else None
        res_p = _moe_dense(yp, g_f, w_route.astype(BF16), w_exp_gate[layer].astype(BF16),
                           w_exp_up[layer].astype(BF16), w_exp_down[layer].astype(BF16), final_g=fin)
        res_s = _moe_dense(ys, g_f, w_route, w_exp_gate[layer], w_exp_up[layer], w_exp_down[layer],
                           precise=True, final_g=fin)
        if fin is None:
            yp, ys = res_p, res_s
        else:
            (yp, y_prompt), (ys, y_sample) = res_p, res_s

    return (y_prompt[None], y_sample[:, None],
            jnp.stack(ret_p), jnp.stack(sbk_p), jnp.stack(sbv_p),
            jnp.stack(foxk_p), jnp.stack(foxv_p), jnp.stack(foxl_p), jnp.stack(gdn_p), jnp.stack(conv_p),
            jnp.stack(memk_p), jnp.stack(memv_p),
            jnp.stack(ret_s), jnp.stack(sbk_s), jnp.stack(sbv_s),
            jnp.stack(foxk_s), jnp.stack(foxv_s), jnp.stack(foxl_s), jnp.stack(gdn_s), jnp.stack(conv_s))
```

```python
import functools
import math

import jax
import jax.numpy as jnp
from jax import lax
from jax.experimental import pallas as pl
from jax.experimental.pallas import tpu as pltpu

F32 = jnp.float32
BF16 = jnp.bfloat16
I32 = jnp.int32

HEAD_DIM = 64
N_HEADS = 8
MIX_HALF = N_HEADS * HEAD_DIM
MEM_HEADS = 4
MEM_HEAD_DIM = 128
N_GROUPS = 4
EXPERTS_PER_GROUP = 4
N_EXPERTS = N_GROUPS * EXPERTS_PER_GROUP
CONV_W = 4
RET_CHUNK = 128
GDN_CHUNK = 64
GDN_INV_BASE = 8
ROPE_BASE = 10000.0
EPS = 1e-6
LOG2E = 1.4426950408889634
LN2 = 0.6931471805599453

LANES = 128
SUBLANES = 8
VMEM_LIMIT_BYTES = 56 * 1024 * 1024

HIGHEST = lax.Precision.HIGHEST


def _cparams(*sem):
    return pltpu.CompilerParams(dimension_semantics=sem, vmem_limit_bytes=VMEM_LIMIT_BYTES)


def _dot(a, b):
    return jnp.dot(a, b, preferred_element_type=F32)


def _dot_nt(a, b):
    return lax.dot_general(a, b, (((1,), (1,)), ((), ())), preferred_element_type=F32)


def _dot_hi(a, b):
    return jnp.dot(a, b, preferred_element_type=F32, precision=HIGHEST)


def _dot_nt_hi(a, b):
    return lax.dot_general(a, b, (((1,), (1,)), ((), ())), preferred_element_type=F32, precision=HIGHEST)


def _split2(x):
    hi = x.astype(BF16)
    lo = (x - hi.astype(F32)).astype(BF16)
    return hi, lo


def _split3(x):
    hi = x.astype(BF16)
    r = x - hi.astype(F32)
    mid = r.astype(BF16)
    lo = (r - mid.astype(F32)).astype(BF16)
    return hi, mid, lo


def _dot_x3(x, m_bf16):
    hi, mid, lo = _split3(x)
    return _dot(hi, m_bf16) + _dot(mid, m_bf16) + _dot(lo, m_bf16)


def _dot_3x(m_bf16, x):
    hi, mid, lo = _split3(x)
    return _dot(m_bf16, hi) + _dot(m_bf16, mid) + _dot(m_bf16, lo)


def _rms(x, g):
    return x * lax.rsqrt(jnp.mean(x * x, axis=-1, keepdims=True) + EPS) * g


def _silu(x):
    return x / (1.0 + jnp.exp(-x))


def _softplus(x):
    return jnp.maximum(x, 0.0) + jnp.log(1.0 + jnp.exp(-jnp.abs(x)))


def _iota(shape, dim):
    return lax.broadcasted_iota(I32, shape, dim)


def _head_block_ones(n, dtype=BF16):
    r = _iota((n, n), 0) // HEAD_DIM
    c = _iota((n, n), 1) // HEAD_DIM
    return jnp.where(r == c, 1.0, 0.0).astype(dtype)


def _norm_proj(x, g, w, outs, *, wt=None, precise=False, tm=512):
    m, d = x.shape
    tm = min(tm, m)
    assert m % tm == 0
    n_out = sum(len(o[2]) for o in outs)

    def body(*refs):
        x_ref, g_ref, w_ref = refs[:3]
        pos = 3
        wt_ref = None
        if wt is not None:
            wt_ref = refs[3]
            pos = 4
        o_refs = refs[pos:]
        xn = _rms(x_ref[...], g_ref[...])
        xb = xn if precise else xn.astype(BF16)
        k = 0
        for start, width, variants in outs:
            wslab = w_ref[:, start:start + width]
            r = _dot_hi(xb, wslab) if precise else _dot(xb, wslab)
            for dtype, scale in variants:
                o_refs[k][...] = (r if scale == 1.0 else r * scale).astype(dtype)
                k += 1
        if wt_ref is not None:
            o_refs[k][...] = _dot_nt_hi(wt_ref[...], xb) if precise else _dot_nt(wt_ref[...], xb)

    in_specs = [pl.BlockSpec((tm, d), lambda i: (i, 0)),
                pl.BlockSpec((1, d), lambda i: (0, 0)),
                pl.BlockSpec(w.shape, lambda i: (0, 0))]
    args = [x, g, w]
    if wt is not None:
        in_specs.append(pl.BlockSpec(wt.shape, lambda i: (0, 0)))
        args.append(wt)
    out_shape, out_specs = [], []
    for start, width, variants in outs:
        for dtype, _ in variants:
            out_shape.append(jax.ShapeDtypeStruct((m, width), dtype))
            out_specs.append(pl.BlockSpec((tm, width), lambda i: (i, 0)))
    if wt is not None:
        out_shape.append(jax.ShapeDtypeStruct((wt.shape[0], m), F32))
        out_specs.append(pl.BlockSpec((wt.shape[0], tm), lambda i: (0, i)))
    res = pl.pallas_call(
        body, grid=(m // tm,), in_specs=in_specs, out_specs=out_specs, out_shape=out_shape,
        compiler_params=_cparams("parallel"), name="norm_proj")(*args)
    assert len(res) == n_out + (wt is not None)
    return res


def _resid_proj(res, a_list, w_list, *, precise=False, tm=512, final_g=None):
    m, d = res.shape
    tm = min(tm, m)
    n = len(a_list)

    def body(*refs):
        r_ref = refs[0]
        a_refs = refs[1:1 + n]
        w_refs = refs[1 + n:1 + 2 * n]
        pos = 1 + 2 * n
        acc = r_ref[...]
        for a_ref, w_ref in zip(a_refs, w_refs):
            acc = acc + (_dot_hi(a_ref[...], w_ref[...]) if precise else _dot(a_ref[...], w_ref[...]))
        if final_g is None:
            refs[pos][...] = acc
        else:
            refs[pos + 1][...] = acc
            refs[pos + 2][...] = _rms(acc, refs[pos][...])

    in_specs = [pl.BlockSpec((tm, d), lambda i: (i, 0))]
    in_specs += [pl.BlockSpec((tm, a.shape[1]), lambda i: (i, 0)) for a in a_list]
    in_specs += [pl.BlockSpec(w.shape, lambda i: (0, 0)) for w in w_list]
    args = [res, *a_list, *w_list]
    out_shape = [jax.ShapeDtypeStruct((m, d), F32)]
    out_specs = [pl.BlockSpec((tm, d), lambda i: (i, 0))]
    if final_g is not None:
        in_specs.append(pl.BlockSpec((1, d), lambda i: (0, 0)))
        args.append(final_g)
        out_shape.append(jax.ShapeDtypeStruct((m, d), F32))
        out_specs.append(pl.BlockSpec((tm, d), lambda i: (i, 0)))
    out = pl.pallas_call(
        body, grid=(m // tm,), in_specs=in_specs, out_specs=out_specs, out_shape=out_shape,
        compiler_params=_cparams("parallel"), name="resid_proj")(*args)
    return out[0] if final_g is None else out


def _rope_tables(pos, inv_lane):
    ang = pos * inv_lane
    return jnp.cos(ang), jnp.sin(ang)


def _rotary_lanes(x, cos, sin, first_half):
    w = x.shape[-1]
    reps = w // LANES
    c = jnp.concatenate([cos] * reps, axis=-1) if reps > 1 else cos
    s = jnp.concatenate([sin] * reps, axis=-1) if reps > 1 else sin
    x_up = pltpu.roll(x, w - HEAD_DIM // 2, 1)
    x_dn = pltpu.roll(x, HEAD_DIM // 2, 1)
    return x * c + jnp.where(first_half, -x_up, x_dn) * s


def _retention_prompt(rq, rk, rv, rg, g_ret, inv_lane):
    t, w = rq.shape
    c = math.gcd(t, RET_CHUNK)
    n = t // c
    scale = HEAD_DIM ** -0.5

    def body(q_ref, k_ref, v_ref, gate_ref, gret_ref, inv_ref, o_ref, s_out_ref,
             s_scr, dec_scr, qd_scr, kd_scr):
        i = pl.program_id(0)
        lane = _iota((c, w), 1)
        head_of_lane = lane // HEAD_DIM
        first_half = (lane % HEAD_DIM) < (HEAD_DIM // 2)
        lg_lane = jnp.log(1.0 - jnp.exp2(-5.0 - head_of_lane.astype(F32)))
        row = _iota((c, w), 0).astype(F32)

        @pl.when(i == 0)
        def _():
            s_scr[...] = jnp.zeros_like(s_scr)
            qd_scr[...] = jnp.exp((row + 1.0) * lg_lane)
            kd_scr[...] = jnp.exp((c - 1.0 - row) * lg_lane)
            ri = _iota((c, c), 0)
            ci = _iota((c, c), 1)
            diff = (ri - ci).astype(F32)
            for h in range(N_HEADS):
                lg_h = math.log(1.0 - 2.0 ** (-5.0 - h))
                dec_scr[h] = jnp.where(ri >= ci, jnp.exp(jnp.where(ri >= ci, diff, 0.0) * lg_h), 0.0)

        pos = (i * c).astype(F32) + _iota((c, 1), 0).astype(F32)
        cos, sin = _rope_tables(pos, inv_ref[...])
        q = _rotary_lanes(q_ref[...], cos, sin, first_half)
        k = _rotary_lanes(k_ref[...], cos, sin, first_half) * scale
        qb = q.astype(BF16)
        kb = k.astype(BF16)
        vb = v_ref[...].astype(BF16)
        s_old = s_scr[...]
        o = _dot(qb, s_old.astype(BF16)) * qd_scr[...]
        hms = [head_of_lane == h for h in range(N_HEADS)]
        scs = [_dot_nt(jnp.where(hm, qb, jnp.zeros_like(qb)), kb) for hm in hms]
        ohs = [_dot((scs[h] * dec_scr[h]).astype(BF16), vb) for h in range(N_HEADS)]
        for h in range(N_HEADS):
            o = o + jnp.where(hms[h], ohs[h], 0.0)
        kd = (k * kd_scr[...]).astype(BF16)
        upd = lax.dot_general(kd, vb, (((0,), (0,)), ((), ())), preferred_element_type=F32)
        blk = (_iota((w, w), 0) // HEAD_DIM) == (_iota((w, w), 1) // HEAD_DIM)
        cd = jnp.exp(float(c) * jnp.log(1.0 - jnp.exp2(-5.0 - (_iota((1, w), 1) // HEAD_DIM).astype(F32))))
        s_scr[...] = s_old * cd + jnp.where(blk, upd, 0.0)
        ones = _head_block_ones(w)
        inv_n = 1.0 / HEAD_DIM
        hi, lo = _split2(o)
        mu = (_dot(hi, ones) + _dot(lo, ones)) * inv_n
        dlt = o - mu
        hi, lo = _split2(dlt * dlt)
        var = (_dot(hi, ones) + _dot(lo, ones)) * inv_n
        y = dlt * lax.rsqrt(var + EPS) * gret_ref[...]
        o_ref[...] = (y * _silu(gate_ref[...])).astype(o_ref.dtype)

        @pl.when(i == n - 1)
        def _():
            s_fin = s_scr[...]
            for h in range(N_HEADS):
                s_out_ref[h] = s_fin[h * HEAD_DIM:(h + 1) * HEAD_DIM, h * HEAD_DIM:(h + 1) * HEAD_DIM]

    tok = pl.BlockSpec((c, w), lambda i: (i, 0))
    o, s = pl.pallas_call(
        body, grid=(n,),
        in_specs=[tok, tok, tok, tok, pl.BlockSpec((1, w), lambda i: (0, 0)),
                  pl.BlockSpec((1, LANES), lambda i: (0, 0))],
        out_specs=[tok, pl.BlockSpec((N_HEADS, HEAD_DIM, HEAD_DIM), lambda i: (0, 0, 0))],
        out_shape=[jax.ShapeDtypeStruct((t, w), BF16),
                   jax.ShapeDtypeStruct((N_HEADS, HEAD_DIM, HEAD_DIM), F32)],
        scratch_shapes=[pltpu.VMEM((w, w), F32), pltpu.VMEM((N_HEADS, c, c), F32),
                        pltpu.VMEM((c, w), F32), pltpu.VMEM((c, w), F32)],
        compiler_params=_cparams("arbitrary"), name="retention_prompt")(rq, rk, rv, rg, g_ret, inv_lane)
    return o, s


ATTN_BLOCKS = (512, 256, 128)
SB_CUM_BLOCK = 256


def _attn_block(t):
    for b in ATTN_BLOCKS:
        if t % b == 0:
            return b
    raise ValueError(f"sequence length {t} must be a multiple of {ATTN_BLOCKS[-1]}")


SOFTPLUS2_LINEAR_ABOVE = 60.0


def _softplus2(z):
    return jnp.where(z > SOFTPLUS2_LINEAR_ABOVE, z, jnp.log(1.0 + jnp.exp2(z)) * LOG2E)


def _sb_prompt(qb, kb, vb):
    t, w = qb.shape
    blk = _attn_block(t)
    cb = min(blk, SB_CUM_BLOCK)
    nblk = blk // cb
    nq = t // blk
    n_pairs = w // LANES

    def body(q_ref, k_ref, v_ref, o_ref, acc_scr, c_scr):
        qi = pl.program_id(1)
        q = q_ref[...]
        lane = _iota((blk, LANES), 1)
        zero = jnp.zeros_like(q)
        q_heads = (jnp.where(lane < HEAD_DIM, q, zero), jnp.where(lane >= HEAD_DIM, q, zero))
        incl = jnp.where(_iota((2 * cb, cb), 0) % cb >= _iota((2 * cb, cb), 1), 1.0, 0.0).astype(BF16)
        acc_scr[...] = jnp.zeros_like(acc_scr)
        c_scr[...] = jnp.zeros_like(c_scr)

        heads = range(2)

        def logits(start):
            k = k_ref[pl.ds(start, blk), :]
            return [_dot_nt(q_heads[hh], k) for hh in heads]

        def blocks(start, strict):
            zs = logits(start)
            v = v_ref[pl.ds(start, blk), :]
            sps = [_softplus2(z) for z in zs]
            if strict is not None:
                sps = [jnp.where(strict, sp, 0.0) for sp in sps]
            carries = [c_scr[hh] for hh in heads]
            parts = [[None] * nblk for _ in heads]
            for b in reversed(range(nblk)):
                cols = slice(b * cb, (b + 1) * cb)
                laters = []
                for hh in heads:
                    hi, lo = _split2(sps[hh][:, cols])
                    laters.append(_dot(jnp.concatenate([hi, lo], axis=1), incl))
                for hh in heads:
                    parts[hh][b] = jnp.exp2(zs[hh][:, cols] - laters[hh] - carries[hh])
                    carries[hh] = carries[hh] + jnp.sum(sps[hh][:, cols], axis=1, keepdims=True)
            wgts = [p[0] if nblk == 1 else jnp.concatenate(p, axis=1) for p in parts]
            if strict is not None:
                wgts = [jnp.where(strict, wgt, 0.0) for wgt in wgts]
            outs = [_dot(wgt.astype(BF16), v) for wgt in wgts]
            for hh in heads:
                acc_scr[hh] += outs[hh]
                c_scr[hh] = carries[hh]

        blocks(pl.multiple_of(qi * blk, blk), _iota((blk, blk), 1) < _iota((blk, blk), 0))

        def step(j, _):
            blocks(pl.multiple_of((qi - 1 - j) * blk, blk), None)
            return 0

        lax.fori_loop(0, qi, step, 0)
        o_ref[...] = jnp.where(lane < HEAD_DIM, acc_scr[0], acc_scr[1]).astype(o_ref.dtype)

    return pl.pallas_call(
        body, grid=(n_pairs, nq),
        in_specs=[pl.BlockSpec((blk, LANES), lambda p, i: (i, p)),
                  pl.BlockSpec((t, LANES), lambda p, i: (0, p)),
                  pl.BlockSpec((t, LANES), lambda p, i: (0, p))],
        out_specs=pl.BlockSpec((blk, LANES), lambda p, i: (i, p)),
        out_shape=jax.ShapeDtypeStruct((t, w), BF16),
        scratch_shapes=[pltpu.VMEM((2, blk, LANES), F32), pltpu.VMEM((2, blk, 1), F32)],
        compiler_params=_cparams("parallel", "arbitrary"), name="sb_prompt")(qb, kb, vb)


def _fox_prompt(qb, kb, vb, cum_row):
    t, w = qb.shape
    blk = _attn_block(t)
    nq = t // blk
    n_pairs = w // LANES

    def body(q_ref, k_ref, v_ref, ck_ref, o_ref, acc_scr, m_scr, z_scr):
        p = pl.program_id(0)
        qi = pl.program_id(1)
        q = q_ref[...]
        lane = _iota((blk, LANES), 1)
        first = lane < HEAD_DIM
        zero = jnp.zeros_like(q)
        q_heads = (jnp.where(first, q, zero), jnp.where(first, zero, q))
        acc_scr[...] = jnp.zeros_like(acc_scr)
        m_scr[...] = jnp.full_like(m_scr, -jnp.inf)
        rsel = _iota((N_HEADS, blk), 0)
        heads = range(2)

        def logits(start):
            k = k_ref[pl.ds(start, blk), :]
            return [_dot_nt(q_heads[hh], k) for hh in heads]

        def blocks(start, causal, zs, next_start, next_slot):
            z_next = logits(next_start)
            for hh in heads:
                z_scr[next_slot, hh] = z_next[hh]
            v = v_ref[pl.ds(start, blk), :]
            ck_all = ck_ref[:, pl.ds(start, blk)]
            one = jnp.ones_like(v)
            v_aug = (jnp.where(first, v, one), jnp.where(first, one, v))
            ss = []
            for hh in heads:
                ck = jnp.sum(jnp.where(rsel == 2 * p + hh, ck_all, 0.0), axis=0, keepdims=True)
                s = zs[hh] - ck
                ss.append(s if causal is None else jnp.where(causal, s, -jnp.inf))
            maxes = [jnp.max(s, axis=1, keepdims=True) for s in ss]
            m_olds = [m_scr[hh] for hh in heads]
            m_news = [jnp.maximum(m_olds[hh], maxes[hh]) for hh in heads]
            alphas = [jnp.exp2(m_olds[hh] - m_news[hh]) for hh in heads]
            prs = [jnp.exp2(ss[hh] - m_news[hh]).astype(BF16) for hh in heads]
            outs = [_dot(prs[hh], v_aug[hh]) for hh in heads]
            for hh in heads:
                acc_scr[hh] = alphas[hh] * acc_scr[hh] + outs[hh]
                m_scr[hh] = m_news[hh]

        def start_of(block):
            return pl.multiple_of(jnp.maximum(block, 0) * blk, blk)

        blocks(start_of(qi), _iota((blk, blk), 1) <= _iota((blk, blk), 0), logits(start_of(qi)), start_of(qi - 1), 0)

        def step(j, _):
            slot = j % 2
            blocks(start_of(qi - 1 - j), None, [z_scr[slot, hh] for hh in heads], start_of(qi - 2 - j), 1 - slot)
            return 0

        lax.fori_loop(0, qi, step, 0)
        outs = [acc_scr[hh] / pltpu.roll(acc_scr[hh], HEAD_DIM, 1) for hh in heads]
        o_ref[...] = jnp.where(first, outs[0], outs[1]).astype(o_ref.dtype)

    return pl.pallas_call(
        body, grid=(n_pairs, nq),
        in_specs=[pl.BlockSpec((blk, LANES), lambda p, i: (i, p)),
                  pl.BlockSpec((t, LANES), lambda p, i: (0, p)),
                  pl.BlockSpec((t, LANES), lambda p, i: (0, p)),
                  pl.BlockSpec((N_HEADS, t), lambda p, i: (0, 0))],
        out_specs=pl.BlockSpec((blk, LANES), lambda p, i: (i, p)),
        out_shape=jax.ShapeDtypeStruct((t, w), BF16),
        scratch_shapes=[pltpu.VMEM((2, blk, LANES), F32), pltpu.VMEM((2, blk, 1), F32),
                        pltpu.VMEM((2, 2, blk, blk), F32)],
        compiler_params=_cparams("parallel", "arbitrary"), name="fox_prompt")(qb, kb, vb, cum_row)


def _gate_math(x, bias, a_log, idx):
    xb = x + bias
    e = jnp.exp(-jnp.abs(xb))
    l1p = jnp.log(1.0 + e)
    logf = jnp.minimum(xb, 0.0) - l1p
    g = -jnp.exp(a_log) * (jnp.maximum(xb, 0.0) + l1p)
    beta = 1.0 / (1.0 + jnp.exp(-xb))
    h = N_HEADS
    return jnp.where(idx < h, logf, jnp.where(idx < 2 * h, g, jnp.where(idx < 3 * h, beta, 0.0)))


def _gates(small_col, small_row, bias_lane, alog_lane, bias_col, alog_col, *, tm=2048):
    m = small_col.shape[0]
    r = small_row.shape[0]
    tm = min(tm, m)

    def body(c_ref, r_ref, bl_ref, al_ref, bc_ref, ac_ref, oc_ref, or_ref):
        oc_ref[...] = _gate_math(c_ref[...], bl_ref[...], al_ref[...], _iota((tm, LANES), 1))
        or_ref[...] = _gate_math(r_ref[...], bc_ref[...], ac_ref[...], _iota((r, tm), 0))

    return pl.pallas_call(
        body, grid=(m // tm,),
        in_specs=[pl.BlockSpec((tm, LANES), lambda i: (i, 0)), pl.BlockSpec((r, tm), lambda i: (0, i)),
                  pl.BlockSpec((1, LANES), lambda i: (0, 0)), pl.BlockSpec((1, LANES), lambda i: (0, 0)),
                  pl.BlockSpec((r, 1), lambda i: (0, 0)), pl.BlockSpec((r, 1), lambda i: (0, 0))],
        out_specs=[pl.BlockSpec((tm, LANES), lambda i: (i, 0)), pl.BlockSpec((r, tm), lambda i: (0, i))],
        out_shape=[jax.ShapeDtypeStruct((m, LANES), F32), jax.ShapeDtypeStruct((r, m), F32)],
        compiler_params=_cparams("parallel"), name="gates")(
            small_col, small_row, bias_lane, alog_lane, bias_col, alog_col)


def _cum_forget(logf_rows, n_chunks):
    r = logf_rows.shape[0]

    def body(x_ref, o_ref):
        x = x_ref[...]
        incl = jnp.where(_iota((LANES, LANES), 0) <= _iota((LANES, LANES), 1), 1.0, 0.0).astype(BF16)
        local = _dot_x3(x, incl)
        tot = jnp.broadcast_to(local[:, LANES - 1:LANES], (r, LANES))
        ri = _iota((r, r), 0)
        ci = _iota((r, r), 1)
        before = jnp.where((ri // n_chunks == ci // n_chunks) & (ci < ri), 1.0, 0.0).astype(BF16)
        o_ref[...] = (local + _dot_3x(before, tot)) * LOG2E

    return pl.pallas_call(
        body, grid=(1,),
        in_specs=[pl.BlockSpec((r, LANES), lambda i: (0, 0))],
        out_specs=pl.BlockSpec((r, LANES), lambda i: (0, 0)),
        out_shape=jax.ShapeDtypeStruct((r, LANES), F32),
        compiler_params=_cparams("arbitrary"), name="cum_forget")(logf_rows)


def _gdn_prep(dqkv, conv_w, *, tm=256):
    t, w3 = dqkv.shape
    w = w3 // 3
    tm = min(tm, t)
    scale = HEAD_DIM ** -0.5

    def body(x_ref, prev_ref, cw_ref, q_ref, k_ref, kt_ref, v_ref):
        i = pl.program_id(0)
        x = x_ref[...]
        prev = jnp.where(i == 0, 0.0, prev_ref[...])
        xp = jnp.concatenate([prev, x], axis=0)
        cw = cw_ref[...]
        off = SUBLANES - (CONV_W - 1)
        y = xp[off:off + tm] * cw[0:1]
        for j in range(1, CONV_W):
            y = y + xp[off + j:off + j + tm] * cw[j:j + 1]
        y = _silu(y)
        ones = _head_block_ones(w)

        def l2n(a):
            hi, lo = _split2(a * a)
            return a * lax.rsqrt(_dot(hi, ones) + _dot(lo, ones) + EPS)

        qn = l2n(y[:, :w]) * scale
        kn = l2n(y[:, w:2 * w])
        vv = y[:, 2 * w:]
        for h in range(N_HEADS):
            sl = slice(h * HEAD_DIM, (h + 1) * HEAD_DIM)
            q_ref[h] = qn[:, sl]
            k_ref[h] = kn[:, sl]
            v_ref[h] = vv[:, sl]
        kt_ref[...] = kn.T

    hm = pl.BlockSpec((N_HEADS, tm, HEAD_DIM), lambda i: (0, i, 0))
    hm_shape = jax.ShapeDtypeStruct((N_HEADS, t, HEAD_DIM), F32)
    return pl.pallas_call(
        body, grid=(t // tm,),
        in_specs=[pl.BlockSpec((tm, w3), lambda i: (i, 0)),
                  pl.BlockSpec((SUBLANES, w3), lambda i: (jnp.maximum(i * (tm // SUBLANES) - 1, 0), 0)),
                  pl.BlockSpec((CONV_W, w3), lambda i: (0, 0))],
        out_specs=[hm, hm, pl.BlockSpec((w, tm), lambda i: (0, i)), hm],
        out_shape=[hm_shape, hm_shape, jax.ShapeDtypeStruct((w, t), F32), hm_shape],
        compiler_params=_cparams("parallel"), name="gdn_prep")(dqkv, dqkv, conv_w)


def _mm3(a, b):
    ah, al = _split2(a)
    bh, bl = _split2(b)
    return _dot(ah, bh) + _dot(ah, bl) + _dot(al, bh)


def _gdn_wy(q, k, kt, v, post_col, post_row):
    nh, t, d = q.shape
    c = math.gcd(t, GDN_CHUNK)
    span = LANES if t % LANES == 0 else t
    per = span // c

    def body(q_ref, k_ref, kt_ref, v_ref, pc_ref, pr_ref, u_ref, w_ref, qe_ref, qk_ref, kdt_ref, egl_ref):
        ri = _iota((c, c), 0)
        ci = _iota((c, c), 1)
        tril = ri >= ci
        strict = ri > ci
        lower_incl = jnp.where(tril, 1.0, 0.0).astype(BF16)
        upper_incl = jnp.where(ri <= ci, 1.0, 0.0).astype(BF16)
        xs, ys = [], []
        units = [(s, h) for s in range(per) for h in range(nh)]
        cums = []
        for s in range(per):
            rows = slice(s * c, (s + 1) * c)
            cums.append((_dot_3x(lower_incl, pc_ref[rows, :]),
                         _dot_x3(pr_ref[:, rows], upper_incl)))
        for s, h in units:
            rows = slice(s * c, (s + 1) * c)
            cg_col_all, cg_row_all = cums[s]
            cgc = cg_col_all[:, nh + h:nh + h + 1]
            cgr = cg_row_all[nh + h:nh + h + 1, :]
            beta = pc_ref[rows, 2 * nh + h:2 * nh + h + 1]
            g_last = cgr[:, c - 1:c]
            decay = jnp.where(tril, jnp.exp(jnp.where(tril, cgc - cgr, 0.0)), 0.0)
            qh = q_ref[h, rows, :]
            kth = kt_ref[h * d:(h + 1) * d, rows]
            k_beta = k_ref[h, rows, :] * beta
            gram = _dot(jnp.concatenate([k_beta, qh], axis=0).astype(BF16), kth.astype(BF16))
            qk_ref[h, rows, :] = jnp.where(tril, gram[c:] * decay, 0.0).astype(BF16)
            qe_ref[h, rows, :] = (qh * jnp.exp(cgc)).astype(BF16)
            kdt_ref[h * d:(h + 1) * d, rows] = (kth * jnp.exp(g_last - cgr)).astype(BF16)
            egl_ref[h, rows, :] = jnp.broadcast_to(jnp.exp(g_last), (c, d))
            xs.append(jnp.where(strict, gram[:c] * decay, 0.0))
            ys.append(jnp.concatenate([v_ref[h, rows, :] * beta, k_beta * jnp.exp(cgc)], axis=1))
        base = GDN_INV_BASE
        assert c % base == 0 and (c // base) & (c // base - 1) == 0
        eye = jnp.where(ri == ci, 1.0, 0.0)
        n_u = len(units)

        def same_block(size):
            return (ri // size) == (ci // size)

        ts = [eye - jnp.where(same_block(base), a, 0.0) for a in xs]
        pws = [t - eye for t in ts]
        span_done = 2
        while span_done < base:
            sq = [_mm3(pws[i], pws[i]) for i in range(n_u)]
            ts = [ts[i] + _mm3(ts[i], sq[i]) for i in range(n_u)]
            pws = sq
            span_done *= 2
        size = base
        while 2 * size < c:
            coup = same_block(2 * size) & ~same_block(size)
            prods = [_mm3(jnp.where(coup, xs[i], 0.0), ts[i]) for i in range(n_u)]
            ts = [ts[i] - _mm3(ts[i], prods[i]) for i in range(n_u)]
            size *= 2
        if size < c:
            coup = ~same_block(size)
            tys = [_mm3(ts[i], ys[i]) for i in range(n_u)]
            prods = [_mm3(jnp.where(coup, xs[i], 0.0), tys[i]) for i in range(n_u)]
            ys = [tys[i] - _mm3(ts[i], prods[i]) for i in range(n_u)]
        else:
            ys = [_mm3(ts[i], ys[i]) for i in range(n_u)]
        for i, (s, h) in enumerate(units):
            rows = slice(s * c, (s + 1) * c)
            u_ref[h, rows, :] = ys[i][:, :d]
            w_ref[h, rows, :] = ys[i][:, d:].astype(BF16)

    hm = pl.BlockSpec((nh, span, d), lambda i: (0, i, 0))
    ktb = pl.BlockSpec((nh * d, span), lambda i: (0, i))
    return pl.pallas_call(
        body, grid=(t // span,),
        in_specs=[hm, hm, ktb, hm, pl.BlockSpec((span, LANES), lambda i: (i, 0)),
                  pl.BlockSpec((post_row.shape[0], span), lambda i: (0, i))],
        out_specs=[hm, hm, hm, hm, ktb, hm],
        out_shape=[jax.ShapeDtypeStruct((nh, t, d), F32), jax.ShapeDtypeStruct((nh, t, d), BF16),
                   jax.ShapeDtypeStruct((nh, t, d), BF16), jax.ShapeDtypeStruct((nh, t, d), BF16),
                   jax.ShapeDtypeStruct((nh * d, t), BF16), jax.ShapeDtypeStruct((nh, t, d), F32)],
        compiler_params=_cparams("parallel"), name="gdn_wy")(q, k, kt, v, post_col, post_row)


def _gdn_scan(u, w, qe, qk, kdt, egl, dz, g_gdn):
    nh, t, d = u.shape
    c = math.gcd(t, GDN_CHUNK)
    span = LANES if t % LANES == 0 else t
    per = span // c
    n = t // span

    def body(u_ref, w_ref, qe_ref, qk_ref, kdt_ref, egl_ref, dz_ref, gg_ref, o_ref, s_out_ref, s_scr, o_scr):
        i = pl.program_id(0)

        @pl.when(i == 0)
        def _():
            s_scr[...] = jnp.zeros_like(s_scr)

        heads = range(nh)
        states = [s_scr[h] for h in heads]
        for j in range(per):
            rows = slice(j * c, (j + 1) * c)
            sbs = [states[h].astype(BF16) for h in heads]
            ws = [_dot(w_ref[h, rows, :], sbs[h]) for h in heads]
            qs = [_dot(qe_ref[h, rows, :], sbs[h]) for h in heads]
            vbs = [(u_ref[h, rows, :] - ws[h]).astype(BF16) for h in heads]
            os_ = [qs[h] + _dot(qk_ref[h, rows, :], vbs[h]) for h in heads]
            upd = [_dot(kdt_ref[h * d:(h + 1) * d, rows], vbs[h]) for h in heads]
            for h in heads:
                states[h] = states[h] * egl_ref[h, rows, :] + upd[h]
                o = os_[h]
                y = o * lax.rsqrt(jnp.mean(o * o, axis=-1, keepdims=True) + EPS) * gg_ref[...]
                o_scr[rows, h * d:(h + 1) * d] = y
        for h in heads:
            s_scr[h] = states[h]
        o_ref[...] = (o_scr[...] * _silu(dz_ref[...])).astype(o_ref.dtype)

        @pl.when(i == n - 1)
        def _():
            s_out_ref[...] = s_scr[...]

    hm = pl.BlockSpec((nh, span, d), lambda i: (0, i, 0))
    return pl.pallas_call(
        body, grid=(n,),
        in_specs=[hm, hm, hm, hm, pl.BlockSpec((nh * d, span), lambda i: (0, i)), hm,
                  pl.BlockSpec((span, nh * d), lambda i: (i, 0)), pl.BlockSpec((1, d), lambda i: (0, 0))],
        out_specs=[pl.BlockSpec((span, nh * d), lambda i: (i, 0)),
                   pl.BlockSpec((nh, d, d), lambda i: (0, 0, 0))],
        out_shape=[jax.ShapeDtypeStruct((t, nh * d), BF16), jax.ShapeDtypeStruct((nh, d, d), F32)],
        scratch_shapes=[pltpu.VMEM((nh, d, d), F32), pltpu.VMEM((span, nh * d), F32)],
        compiler_params=_cparams("arbitrary"), name="gdn_scan")(u, w, qe, qk, kdt, egl, dz, g_gdn)


def _mem_attn_core(q, mk, mv, precise):
    scale = MEM_HEAD_DIM ** -0.5
    outs = []
    for h in range(MEM_HEADS):
        sl = slice(h * MEM_HEAD_DIM, (h + 1) * MEM_HEAD_DIM)
        if precise:
            s = _dot_nt_hi(q[:, sl], mk[:, sl]) * scale
        else:
            s = _dot_nt(q[:, sl].astype(BF16), mk[:, sl]) * scale
        e = jnp.exp(s - jnp.max(s, axis=-1, keepdims=True))
        p = e / jnp.sum(e, axis=-1, keepdims=True)
        outs.append(_dot_hi(p, mv[:, sl]) if precise else _dot(p.astype(BF16), mv[:, sl]))
    return jnp.concatenate(outs, axis=-1)


def _mem_attn_prompt(y, g, wq, mk, mv, wo, *, tm=512):
    m, d = y.shape
    tm = min(tm, m)

    def body(y_ref, g_ref, wq_ref, mk_ref, mv_ref, wo_ref, o_ref):
        yv = y_ref[...]
        hb = _rms(yv, g_ref[...]).astype(BF16)
        q = _dot(hb, wq_ref[...])
        o = _mem_attn_core(q, mk_ref[...], mv_ref[...], False)
        o_ref[...] = yv + _dot(o.astype(BF16), wo_ref[...])

    full = lambda a: pl.BlockSpec(a.shape, lambda i: (0, 0))
    return pl.pallas_call(
        body, grid=(m // tm,),
        in_specs=[pl.BlockSpec((tm, d), lambda i: (i, 0)), pl.BlockSpec((1, d), lambda i: (0, 0)),
                  full(wq), full(mk), full(mv), full(wo)],
        out_specs=pl.BlockSpec((tm, d), lambda i: (i, 0)),
        out_shape=jax.ShapeDtypeStruct((m, d), F32),
        compiler_params=_cparams("parallel"), name="mem_attn_prompt")(y, g, wq, mk, mv, wo)


def _mem_attn_sample(q, mem_k, mem_v):
    b, _, w = q.shape
    n_mem = mem_k.shape[1]

    def body(q_ref, k_ref, v_ref, o_ref):
        q8 = jnp.broadcast_to(q_ref[0], (SUBLANES, w))
        o = _mem_attn_core(q8, k_ref[0], v_ref[0], True)
        o_ref[0] = o[0:1]

    return pl.pallas_call(
        body, grid=(b,),
        in_specs=[pl.BlockSpec((1, 1, w), lambda i: (i, 0, 0)),
                  pl.BlockSpec((1, n_mem, w), lambda i: (i, 0, 0)),
                  pl.BlockSpec((1, n_mem, w), lambda i: (i, 0, 0))],
        out_specs=pl.BlockSpec((1, 1, w), lambda i: (i, 0, 0)),
        out_shape=jax.ShapeDtypeStruct((b, 1, w), F32),
        compiler_params=_cparams("parallel"), name="mem_attn_sample")(q, mem_k, mem_v)


def _route(logits):
    m = logits.shape[0]
    lane = _iota((m, LANES), 1)
    big = LANES
    neg = -jnp.inf
    is_g = lane < N_GROUPS
    gl = jnp.where(is_g, logits, neg)
    gmax = jnp.max(gl, axis=1, keepdims=True)
    g_val = 1.0 / jnp.sum(jnp.where(is_g, jnp.exp(gl - gmax), 0.0), axis=1, keepdims=True)
    g_idx = jnp.min(jnp.where(is_g & (gl == gmax), lane, big), axis=1, keepdims=True)
    lo = N_GROUPS + g_idx * EXPERTS_PER_GROUP
    in_grp = (lane >= lo) & (lane < lo + EXPERTS_PER_GROUP)
    el = jnp.where(in_grp, logits, neg)
    l1 = jnp.max(el, axis=1, keepdims=True)
    i1 = jnp.min(jnp.where(in_grp & (el == l1), lane, big), axis=1, keepdims=True)
    el2 = jnp.where(lane == i1, neg, el)
    l2 = jnp.max(el2, axis=1, keepdims=True)
    i2 = jnp.min(jnp.where(in_grp & (lane != i1) & (el2 == l2), lane, big), axis=1, keepdims=True)
    p2 = jnp.exp(l2 - l1)
    w1 = g_val / (1.0 + p2)
    w2 = g_val * p2 / (1.0 + p2)
    return jnp.where(lane == i1, w1, jnp.where(lane == i2, w2, 0.0))


def _moe_dense(y, g, w_route, w_gate, w_up, w_down, *, precise=False, tm=1024, final_g=None):
    m, d = y.shape
    tm = min(tm, m)
    ne, _, ff = w_gate.shape
    xdt = F32 if precise else BF16

    def body(*refs):
        y_ref, g_ref, wr_ref, wg_ref, wu_ref, wd_ref = refs[:6]
        pos = 6
        fg_ref = None
        if final_g is not None:
            fg_ref = refs[6]
            pos = 7
        o_ref = refs[pos]
        pos += 1
        on_ref = None
        if final_g is not None:
            on_ref = refs[pos]
            pos += 1
        xn_scr, gate_scr, acc_scr = refs[pos:pos + 3]
        e = pl.program_id(1)

        @pl.when(e == 0)
        def _():
            xn = _rms(y_ref[...], g_ref[...])
            xn_scr[...] = xn.astype(xdt)
            logits = _dot_hi(xn, wr_ref[...]) if precise else _dot(xn.astype(BF16), wr_ref[...])
            gate_scr[...] = _route(logits)
            acc_scr[...] = jnp.zeros_like(acc_scr)

        xn = xn_scr[...]
        if precise:
            hg = _dot_hi(xn, wg_ref[0])
            hu = _dot_hi(xn, wu_ref[0])
            out = _dot_hi(_silu(hg) * hu, wd_ref[0])
        else:
            hg = _dot(xn, wg_ref[0])
            hu = _dot(xn, wu_ref[0])
            out = _dot((_silu(hg) * hu).astype(BF16), wd_ref[0])
        lane = _iota((tm, LANES), 1)
        ge = jnp.sum(jnp.where(lane == N_GROUPS + e, gate_scr[...], 0.0), axis=1, keepdims=True)
        acc_scr[...] += ge * out

        @pl.when(e == ne - 1)
        def _():
            res = y_ref[...] + acc_scr[...]
            o_ref[...] = res
            if on_ref is not None:
                on_ref[...] = _rms(res, fg_ref[...])

    tok = pl.BlockSpec((tm, d), lambda i, e: (i, 0))
    in_specs = [tok, pl.BlockSpec((1, d), lambda i, e: (0, 0)), pl.BlockSpec(w_route.shape, lambda i, e: (0, 0)),
                pl.BlockSpec((1, d, ff), lambda i, e: (e, 0, 0)), pl.BlockSpec((1, d, ff), lambda i, e: (e, 0, 0)),
                pl.BlockSpec((1, ff, d), lambda i, e: (e, 0, 0))]
    args = [y, g, w_route, w_gate, w_up, w_down]
    out_shape = [jax.ShapeDtypeStruct((m, d), F32)]
    out_specs = [tok]
    if final_g is not None:
        in_specs.append(pl.BlockSpec((1, d), lambda i, e: (0, 0)))
        args.append(final_g)
        out_shape.append(jax.ShapeDtypeStruct((m, d), F32))
        out_specs.append(tok)
    res = pl.pallas_call(
        body, grid=(m // tm, ne), in_specs=in_specs, out_specs=out_specs, out_shape=out_shape,
        scratch_shapes=[pltpu.VMEM((tm, d), xdt), pltpu.VMEM((tm, LANES), F32), pltpu.VMEM((tm, d), F32)],
        compiler_params=_cparams("parallel", "arbitrary"), name="moe_dense")(*args)
    return res[0] if final_g is None else res


def _heads3(x):
    return x.reshape(N_HEADS, HEAD_DIM, x.shape[-1])


def _retention_sample(q_col, k_col, v, gate, state, g_ret, inv_col, pos):
    b, w, _ = q_col.shape
    scale = HEAD_DIM ** -0.5
    half = HEAD_DIM // 2

    def body(q_ref, k_ref, v_ref, gate_ref, s_ref, gr_ref, inv_ref, o_ref, so_ref):
        row = _iota((w, 1), 0)
        first_half = (row % HEAD_DIM) < half
        ang = float(pos) * inv_ref[...]
        cos, sin = jnp.cos(ang), jnp.sin(ang)

        def rot(x):
            return x * cos + jnp.where(first_half, -pltpu.roll(x, w - half, 0), pltpu.roll(x, half, 0)) * sin

        q = rot(q_ref[0])
        k = rot(k_ref[0]) * scale
        gamma = 1.0 - jnp.exp2(-5.0 - (row // HEAD_DIM).astype(F32))
        s = s_ref[0]
        vv = v_ref[0]
        qk = jnp.sum(_heads3(q * k), axis=1)
        qs = jnp.sum(_heads3(q * s * gamma), axis=1)
        o = qk * vv + qs
        v_rows = jnp.broadcast_to(vv[:, None, :], (N_HEADS, HEAD_DIM, HEAD_DIM)).reshape(w, HEAD_DIM)
        so_ref[0] = s * gamma + k * v_rows
        mu = jnp.mean(o, axis=-1, keepdims=True)
        dlt = o - mu
        var = jnp.mean(dlt * dlt, axis=-1, keepdims=True)
        o_ref[0] = dlt * lax.rsqrt(var + EPS) * gr_ref[...] * _silu(gate_ref[0])

    col = pl.BlockSpec((1, w, 1), lambda i: (i, 0, 0))
    hd = pl.BlockSpec((1, N_HEADS, HEAD_DIM), lambda i: (i, 0, 0))
    st = pl.BlockSpec((1, w, HEAD_DIM), lambda i: (i, 0, 0))
    return pl.pallas_call(
        body, grid=(b,),
        in_specs=[col, col, hd, hd, st, pl.BlockSpec((N_HEADS, HEAD_DIM), lambda i: (0, 0)),
                  pl.BlockSpec((w, 1), lambda i: (0, 0))],
        out_specs=[hd, st],
        out_shape=[jax.ShapeDtypeStruct((b, N_HEADS, HEAD_DIM), F32), jax.ShapeDtypeStruct((b, w, HEAD_DIM), F32)],
        compiler_params=_cparams("parallel"), name="retention_sample")(q_col, k_col, v, gate, state, g_ret, inv_col)


def _gdn_sample(xqk_col, bufqk_col, cwqk_col, xv, bufv, cwv, dz, small_col, bias_col, alog_col, state, g_gdn):
    b = xqk_col.shape[0]
    w = N_HEADS * HEAD_DIM
    scale = HEAD_DIM ** -0.5
    r = small_col.shape[1]

    def body(x_ref, bq_ref, cq_ref, xv_ref, bv_ref, cv_ref, dz_ref, sm_ref, bc_ref, ac_ref, s_ref, gg_ref,
             o_ref, so_ref):
        yqk = x_ref[0] * cq_ref[CONV_W - 1]
        yv = xv_ref[0] * cv_ref[CONV_W - 1]
        for j in range(CONV_W - 1):
            yqk = yqk + bq_ref[0, j] * cq_ref[j]
            yv = yv + bv_ref[0, j] * cv_ref[j]
        yqk = _silu(yqk)
        vv = _silu(yv)

        def l2n(a):
            a3 = _heads3(a)
            ss = jnp.sum(a3 * a3, axis=1, keepdims=True)
            return a3 * lax.rsqrt(ss + EPS)

        q3 = l2n(yqk[:w]) * scale
        k3 = l2n(yqk[w:])
        post = _gate_math(sm_ref[0], bc_ref[...], ac_ref[...], _iota((r, 1), 0))
        g = post[N_HEADS:2 * N_HEADS]
        beta = post[2 * N_HEADS:3 * N_HEADS]
        a = jnp.exp(g)
        s3 = _heads3(s_ref[0])
        ks = jnp.sum(k3 * s3, axis=1)
        v_new = beta * vv - (beta * a) * ks
        o = a * jnp.sum(q3 * s3, axis=1) + jnp.sum(q3 * k3, axis=1) * v_new
        s_new = s3 * a[:, :, None] + k3 * v_new[:, None, :]
        so_ref[0] = s_new.reshape(w, HEAD_DIM)
        y = o * lax.rsqrt(jnp.mean(o * o, axis=-1, keepdims=True) + EPS) * gg_ref[...]
        o_ref[0] = y * _silu(dz_ref[0])

    hd = pl.BlockSpec((1, N_HEADS, HEAD_DIM), lambda i: (i, 0, 0))
    st = pl.BlockSpec((1, w, HEAD_DIM), lambda i: (i, 0, 0))
    return pl.pallas_call(
        body, grid=(b,),
        in_specs=[pl.BlockSpec((1, 2 * w, 1), lambda i: (i, 0, 0)),
                  pl.BlockSpec((1, CONV_W - 1, 2 * w, 1), lambda i: (i, 0, 0, 0)),
                  pl.BlockSpec((CONV_W, 2 * w, 1), lambda i: (0, 0, 0)),
                  hd, pl.BlockSpec((1, CONV_W - 1, N_HEADS, HEAD_DIM), lambda i: (i, 0, 0, 0)),
                  pl.BlockSpec((CONV_W, N_HEADS, HEAD_DIM), lambda i: (0, 0, 0)),
                  hd, pl.BlockSpec((1, r, 1), lambda i: (i, 0, 0)),
                  pl.BlockSpec((r, 1), lambda i: (0, 0)), pl.BlockSpec((r, 1), lambda i: (0, 0)),
                  st, pl.BlockSpec((1, HEAD_DIM), lambda i: (0, 0))],
        out_specs=[hd, st],
        out_shape=[jax.ShapeDtypeStruct((b, N_HEADS, HEAD_DIM), F32), jax.ShapeDtypeStruct((b, w, HEAD_DIM), F32)],
        compiler_params=_cparams("parallel"), name="gdn_sample")(
            xqk_col, bufqk_col, cwqk_col, xv, bufv, cwv, dz, small_col, bias_col, alog_col, state, g_gdn)


PAGES_PER_STEP = 8


def _paged_sweep(mode, q_col, cache_kt, cache_vt, page_table, page_base, *, k_new=None, v_new=None,
                 cache_lft=None, logf_new=None):
    fox = mode == "fox"
    b, nh, hd, _ = q_col.shape
    n_pages = page_table.shape[1]
    page = cache_kt.shape[-1]
    pps = math.gcd(n_pages, PAGES_PER_STEP)
    ng = n_pages // pps
    qscale = hd ** -0.5 * LOG2E
    n_split = 3 if fox else 2

    def body(*refs):
        q_ref = refs[1]
        pos = 2
        k_refs = refs[pos:pos + pps]
        pos += pps
        v_refs = refs[pos:pos + pps]
        pos += pps
        if fox:
            lf_refs = refs[pos:pos + pps]
            pos += pps
            kn_ref, vn_ref, lfn_ref = refs[pos:pos + 3]
            pos += 3
        o_ref = refs[pos]
        pos += 1
        acc_scr, carry_scr = refs[pos:pos + 2]
        pos += 2
        if fox:
            m_scr, l_scr = refs[pos:pos + 2]
        j = pl.program_id(1)
        q3 = q_ref[0] * qscale
        li = _iota((page, page), 0)
        lj = _iota((page, page), 1)
        later = jnp.where((li > lj) if fox else (li >= lj), 1.0, 0.0).astype(BF16)
        later = jnp.concatenate([later] * n_split, axis=0)

        @pl.when(j == 0)
        def _():
            if fox:
                m_scr[...] = jnp.sum(q3 * kn_ref[0], axis=1)
                l_scr[...] = jnp.ones_like(l_scr)
                acc_scr[...] = jnp.where(_iota((nh, hd, page), 2) == 0, vn_ref[0], 0.0)
                carry_scr[...] = lfn_ref[0] * LOG2E
            else:
                acc_scr[...] = jnp.zeros_like(acc_scr)
                carry_scr[...] = jnp.zeros_like(carry_scr)

        pages = range(pps)
        zs = [jnp.sum(k_refs[p][0] * q3, axis=1) for p in pages]
        carry = carry_scr[...]
        if not fox:
            sps = [_softplus2(z) for z in zs]
            laters = [_dot(jnp.concatenate(_split2(sp), axis=1), later) for sp in sps]
            wts = []
            for p in pages:
                wts.append(jnp.exp2(zs[p] - laters[p] - carry))
                carry = carry + jnp.sum(sps[p], axis=1, keepdims=True)
            acc = acc_scr[...]
        else:
            lfs = [lf_refs[p][0] * LOG2E for p in pages]
            laters = [_dot(jnp.concatenate(_split3(lf), axis=1), later) for lf in lfs]
            logits = []
            for p in pages:
                logits.append(zs[p] + (laters[p] + carry))
                carry = carry + jnp.sum(lfs[p], axis=1, keepdims=True)
            m_old = m_scr[...]
            m_new = m_old
            for s in logits:
                m_new = jnp.maximum(m_new, jnp.max(s, axis=1, keepdims=True))
            alpha = jnp.exp2(m_old - m_new)
            wts = [jnp.exp2(s - m_new) for s in logits]
            l_new = alpha * l_scr[...]
            for wt in wts:
                l_new = l_new + jnp.sum(wt, axis=1, keepdims=True)
            l_scr[...] = l_new
            m_scr[...] = m_new
            acc = acc_scr[...] * alpha[:, :, None]
        carry_scr[...] = carry
        for p in pages:
            acc = acc + wts[p][:, None, :] * v_refs[p][0]
        acc_scr[...] = acc

        @pl.when(j == ng - 1)
        def _():
            res = jnp.sum(acc_scr[...], axis=2, keepdims=True)
            o_ref[0] = res / l_scr[...][:, :, None] if fox else res

    def page_map(p, ndim):
        def index_map(i, j, pt):
            return (page_base + pt[i, (ng - 1 - j) * pps + (pps - 1 - p)],) + (0,) * (ndim - 1)
        return index_map

    col_spec = pl.BlockSpec((1, nh, hd, 1), lambda i, j, pt: (i, 0, 0, 0))
    in_specs = [col_spec]
    in_specs += [pl.BlockSpec((1, nh, hd, page), page_map(p, 4)) for p in range(pps)]
    in_specs += [pl.BlockSpec((1, nh, hd, page), page_map(p, 4)) for p in range(pps)]
    args = [q_col] + [cache_kt] * pps + [cache_vt] * pps
    scratch = [pltpu.VMEM((nh, hd, page), F32), pltpu.VMEM((nh, 1), F32)]
    if fox:
        in_specs += [pl.BlockSpec((1, nh, page), page_map(p, 3)) for p in range(pps)]
        in_specs += [col_spec, col_spec, pl.BlockSpec((1, nh, 1), lambda i, j, pt: (i, 0, 0))]
        args += [cache_lft] * pps + [k_new, v_new, logf_new]
        scratch += [pltpu.VMEM((nh, 1), F32), pltpu.VMEM((nh, 1), F32)]
    return pl.pallas_call(
        body,
        grid_spec=pltpu.PrefetchScalarGridSpec(
            num_scalar_prefetch=1, grid=(b, ng), in_specs=in_specs, out_specs=col_spec, scratch_shapes=scratch),
        out_shape=jax.ShapeDtypeStruct((b, nh, hd, 1), F32),
        compiler_params=_cparams("parallel", "arbitrary"), name="paged_sweep_" + mode)(page_table, *args)


def kernel(x_prompt, x_sample, state_ret, cache_sb_k, cache_sb_v, cache_fox_k, cache_fox_v, cache_fox_logf, state_gdn, state_conv, cache_mem_k, cache_mem_v, page_table, mem_prompt, g_mix, ev_w_in, ev_g_ret, ev_w_out, od_w_in, od_b_forget, od_conv_w, od_a_log, od_dt_bias, od_g_gdn, od_w_out, g_xattn, g_mem, w_mq, w_mk, w_mv, w_mo, g_ffn, w_route_group, w_route_expert, w_exp_gate, w_exp_up, w_exp_down, g_final):
    bp, t, d = x_prompt.shape
    bs, ts, _ = x_sample.shape
    assert bp == 1 and ts == 1 and t >= CONV_W - 1
    depth = g_mix.shape[0]
    n_pages = page_table.shape[1]
    n_phys, page = cache_sb_k.shape[1], cache_sb_k.shape[2]
    past_len = n_pages * page
    w = MIX_HALF
    nh, hd = N_HEADS, HEAD_DIM
    attn_scale = hd ** -0.5 * LOG2E
    f32o = [(F32, 1.0)]
    both = [(F32, 1.0), (BF16, 1.0)]
    row2 = lambda v: v.reshape(1, -1)

    def token_minor(cache):
        return jnp.transpose(cache, (0, 1, 3, 4, 2)).reshape(-1, nh, hd, page)

    half = hd // 2
    inv = ROPE_BASE ** (-jnp.arange(half, dtype=F32) / half)
    inv_lane = jnp.tile(inv, LANES // half)[None]
    inv_col = jnp.tile(inv, w // half)[:, None]

    yp = x_prompt[0]
    ys = x_sample[:, 0]
    y_prompt = y_sample = None
    ret_p, sbk_p, sbv_p, foxk_p, foxv_p, foxl_p, gdn_p, conv_p, memk_p, memv_p = ([] for _ in range(10))
    ret_s, sbk_s, sbv_s, foxk_s, foxv_s, foxl_s, gdn_s, conv_s = ([] for _ in range(8))

    for layer in range(depth):
        i = layer // 2
        g_in = row2(g_mix[layer])
        if layer % 2 == 0:
            w_in = ev_w_in[i]
            w_out = ev_w_out[i]
            outs = [(0, w, f32o), (w, w, f32o), (2 * w, w, f32o), (3 * w, w, f32o),
                    (4 * w, w, [(BF16, attn_scale)]), (5 * w, w, both), (6 * w, w, both)]
            rq, rk, rv, rg, sqb, sk, skb, sv, svb = _norm_proj(yp, g_in, w_in.astype(BF16), outs)
            o_ret, s_ret = _retention_prompt(rq, rk, rv, rg, row2(ev_g_ret[i]), inv_lane)
            o_sb = _sb_prompt(sqb, skb, svb)
            w_out_b = w_out.astype(BF16)
            yp = _resid_proj(yp, [o_ret, o_sb], [w_out_b[:w], w_out_b[w:]])
            ret_p.append(s_ret[None])
            sbk_p.append(sk.reshape(1, t, nh, hd))
            sbv_p.append(sv.reshape(1, t, nh, hd))
            outs = [(c * w, w, f32o) for c in range(7)]
            rq, rk, rv, rg, sq, sk, sv = _norm_proj(ys, g_in, w_in, outs, precise=True)
            o_ret, s_new = _retention_sample(
                rq.reshape(bs, w, 1), rk.reshape(bs, w, 1), rv.reshape(bs, nh, hd), rg.reshape(bs, nh, hd),
                state_ret[i].reshape(bs, w, hd), ev_g_ret[i].reshape(nh, hd), inv_col, past_len)
            o_sb = _paged_sweep("sb", sq.reshape(bs, nh, hd, 1), token_minor(cache_sb_k), token_minor(cache_sb_v),
                                page_table, i * n_phys)
            ys = _resid_proj(ys, [o_ret.reshape(bs, w), o_sb.reshape(bs, w)], [w_out[:w], w_out[w:]], precise=True)
            ret_s.append(s_new.reshape(bs, nh, hd, hd))
            sbk_s.append(sk.reshape(bs, 1, nh, hd))
            sbv_s.append(sv.reshape(bs, 1, nh, hd))
        else:
            w_full = od_w_in[i]
            w_out = od_w_out[i]
            c_fl = 3 * w
            c_qkv = c_fl + nh
            c_dz = c_qkv + 3 * w
            c_da = c_dz + w
            n_small = 3 * nh
            small_w = jnp.concatenate([w_full[:, c_fl:c_qkv], w_full[:, c_da:c_da + 2 * nh],
                                       jnp.zeros((d, LANES - n_small), F32)], axis=1)
            w_in = jnp.concatenate([w_full[:, :c_fl], w_full[:, c_qkv:c_da], small_w], axis=1)
            o_dqkv, o_dz, o_small = 3 * w, 6 * w, 7 * w
            zeros8 = jnp.zeros((nh,), F32)
            bias = jnp.concatenate([od_b_forget[i], od_dt_bias[i], zeros8])
            alog = jnp.concatenate([zeros8, od_a_log[i], zeros8])
            n_rows = 4 * nh
            bias_lane = jnp.pad(bias, (0, LANES - n_small))[None]
            alog_lane = jnp.pad(alog, (0, LANES - n_small))[None]
            bias_col = jnp.pad(bias, (0, n_rows - n_small))[:, None]
            alog_col = jnp.pad(alog, (0, n_rows - n_small))[:, None]
            outs = [(0, w, [(BF16, attn_scale)]), (w, w, both), (2 * w, w, both),
                    (o_dqkv, 3 * w, f32o), (o_dz, w, f32o), (o_small, LANES, f32o)]
            w_in_b = w_in.astype(BF16)
            small_wt = small_w[:, :n_rows].T.astype(BF16)
            fqb, fk, fkb, fv, fvb, dqkv, dz, small_col, small_row = _norm_proj(yp, g_in, w_in_b, outs, wt=small_wt)
            post_col, post_row = _gates(small_col, small_row, bias_lane, alog_lane, bias_col, alog_col)
            n_chunks = t // LANES
            cum_row = _cum_forget(post_row[:nh].reshape(nh * n_chunks, LANES), n_chunks).reshape(nh, t)
            o_fox = _fox_prompt(fqb, fkb, fvb, cum_row)
            gq, gk, gkt, gv = _gdn_prep(dqkv, od_conv_w[i])
            wy = _gdn_wy(gq, gk, gkt, gv, post_col, post_row)
            o_gdn, s_gdn = _gdn_scan(*wy, dz, row2(od_g_gdn[i]))
            w_out_b = w_out.astype(BF16)
            yp = _resid_proj(yp, [o_fox, o_gdn], [w_out_b[:w], w_out_b[w:]])
            foxk_p.append(fk.reshape(1, t, nh, hd))
            foxv_p.append(fv.reshape(1, t, nh, hd))
            foxl_p.append(post_col[:, :nh].reshape(1, t, nh))
            gdn_p.append(s_gdn[None])
            conv_p.append(dqkv[t - (CONV_W - 1):][None])
            outs = [(0, w, f32o), (w, w, f32o), (2 * w, w, f32o), (o_dqkv, 3 * w, f32o), (o_dz, w, f32o),
                    (o_small, LANES, f32o)]
            fq, fk, fv, dqkv, dz, small = _norm_proj(ys, g_in, w_in, outs, precise=True)
            post_s, _ = _gates(small, small[:, :n_rows].T, bias_lane, alog_lane, bias_col, alog_col)
            o_fox = _paged_sweep(
                "fox", fq.reshape(bs, nh, hd, 1), token_minor(cache_fox_k), token_minor(cache_fox_v),
                page_table, i * n_phys, k_new=fk.reshape(bs, nh, hd, 1), v_new=fv.reshape(bs, nh, hd, 1),
                cache_lft=jnp.swapaxes(cache_fox_logf, 2, 3).reshape(-1, nh, page),
                logf_new=post_s[:, :nh].reshape(bs, nh, 1))
            buf = state_conv[i]
            cw = od_conv_w[i]
            o_gdn, s_new = _gdn_sample(
                dqkv[:, :2 * w].reshape(bs, 2 * w, 1), buf[:, :, :2 * w].reshape(bs, CONV_W - 1, 2 * w, 1),
                cw[:, :2 * w].reshape(CONV_W, 2 * w, 1), dqkv[:, 2 * w:].reshape(bs, nh, hd),
                buf[:, :, 2 * w:].reshape(bs, CONV_W - 1, nh, hd), cw[:, 2 * w:].reshape(CONV_W, nh, hd),
                dz.reshape(bs, nh, hd), small[:, :n_rows].reshape(bs, n_rows, 1), bias_col, alog_col,
                state_gdn[i].reshape(bs, w, hd), row2(od_g_gdn[i]))
            ys = _resid_proj(ys, [o_fox.reshape(bs, w), o_gdn.reshape(bs, w)], [w_out[:w], w_out[w:]], precise=True)
            foxk_s.append(fk.reshape(bs, 1, nh, hd))
            foxv_s.append(fv.reshape(bs, 1, nh, hd))
            foxl_s.append(post_s[:, :nh].reshape(bs, 1, nh))
            gdn_s.append(s_new.reshape(bs, nh, hd, hd))
            conv_s.append(jnp.concatenate([buf[:, 1:], dqkv[:, None]], axis=1))

        n_mem = mem_prompt.shape[1]
        mw = MEM_HEADS * MEM_HEAD_DIM
        w_kv = jnp.concatenate([w_mk[layer], w_mv[layer]], axis=1).astype(BF16)
        mk, mkb, mv, mvb = _norm_proj(mem_prompt[0], row2(g_mem[layer]), w_kv, [(0, mw, both), (mw, mw, both)])
        memk_p.append(mk.reshape(1, n_mem, MEM_HEADS, MEM_HEAD_DIM))
        memv_p.append(mv.reshape(1, n_mem, MEM_HEADS, MEM_HEAD_DIM))
        g_x = row2(g_xattn[layer])
        yp = _mem_attn_prompt(yp, g_x, w_mq[layer].astype(BF16), mkb, mvb, w_mo[layer].astype(BF16))
        (q_s,) = _norm_proj(ys, g_x, w_mq[layer], [(0, mw, f32o)], precise=True)
        o_mem = _mem_attn_sample(q_s.reshape(bs, 1, mw), cache_mem_k[layer].reshape(bs, n_mem, mw),
                                 cache_mem_v[layer].reshape(bs, n_mem, mw))
        ys = _resid_proj(ys, [o_mem.reshape(bs, mw)], [w_mo[layer]], precise=True)

        w_route = jnp.concatenate([w_route_group[layer], w_route_expert[layer],
                                   jnp.zeros((d, LANES - N_GROUPS - N_EXPERTS), F32)], axis=1)
        g_f = row2(g_ffn[layer])
        fin = row2(g_final) if layer == depth - 1 else None
        res_p = _moe_dense(yp, g_f, w_route.astype(BF16), w_exp_gate[layer].astype(BF16),
                           w_exp_up[layer].astype(BF16), w_exp_down[layer].astype(BF16), final_g=fin)
        res_s = _moe_dense(ys, g_f, w_route, w_exp_gate[layer], w_exp_up[layer], w_exp_down[layer],
                           precise=True, final_g=fin)
        if fin is None:
            yp, ys = res_p, res_s
        else:
            (yp, y_prompt), (ys, y_sample) = res_p, res_s

    return (y_prompt[None], y_sample[:, None],
            jnp.stack(ret_p), jnp.stack(sbk_p), jnp.stack(sbv_p),
            jnp.stack(foxk_p), jnp.stack(foxv_p), jnp.stack(foxl_p), jnp.stack(gdn_p), jnp.stack(conv_p),
            jnp.stack(memk_p), jnp.stack(memv_p),
            jnp.stack(ret_s), jnp.stack(sbk_s), jnp.stack(sbv_s),
            jnp.stack(foxk_s), jnp.stack(foxv_s), jnp.stack(foxl_s), jnp.stack(gdn_s), jnp.stack(conv_s))
```

```python
import functools
import math

import jax
import jax.numpy as jnp
from jax import lax
from jax.experimental import pallas as pl
from jax.experimental.pallas import tpu as pltpu

F32 = jnp.float32
BF16 = jnp.bfloat16
I32 = jnp.int32

HEAD_DIM = 64
N_HEADS = 8
MIX_HALF = N_HEADS * HEAD_DIM
MEM_HEADS = 4
MEM_HEAD_DIM = 128
N_GROUPS = 4
EXPERTS_PER_GROUP = 4
N_EXPERTS = N_GROUPS * EXPERTS_PER_GROUP
CONV_W = 4
RET_CHUNK = 128
GDN_CHUNK = 64
GDN_INV_BASE = 8
ROPE_BASE = 10000.0
EPS = 1e-6
LOG2E = 1.4426950408889634
LN2 = 0.6931471805599453

LANES = 128
SUBLANES = 8
VMEM_LIMIT_BYTES = 56 * 1024 * 1024

HIGHEST = lax.Precision.HIGHEST


def _cparams(*sem):
    return pltpu.CompilerParams(dimension_semantics=sem, vmem_limit_bytes=VMEM_LIMIT_BYTES)


def _dot(a, b):
    return jnp.dot(a, b, preferred_element_type=F32)


def _dot_nt(a, b):
    return lax.dot_general(a, b, (((1,), (1,)), ((), ())), preferred_element_type=F32)


def _dot_hi(a, b):
    return jnp.dot(a, b, preferred_element_type=F32, precision=HIGHEST)


def _dot_nt_hi(a, b):
    return lax.dot_general(a, b, (((1,), (1,)), ((), ())), preferred_element_type=F32, precision=HIGHEST)


def _split2(x):
    hi = x.astype(BF16)
    lo = (x - hi.astype(F32)).astype(BF16)
    return hi, lo


def _split3(x):
    hi = x.astype(BF16)
    r = x - hi.astype(F32)
    mid = r.astype(BF16)
    lo = (r - mid.astype(F32)).astype(BF16)
    return hi, mid, lo


def _dot_x3(x, m_bf16):
    hi, mid, lo = _split3(x)
    return _dot(hi, m_bf16) + _dot(mid, m_bf16) + _dot(lo, m_bf16)


def _dot_3x(m_bf16, x):
    hi, mid, lo = _split3(x)
    return _dot(m_bf16, hi) + _dot(m_bf16, mid) + _dot(m_bf16, lo)


def _rms(x, g):
    return x * lax.rsqrt(jnp.mean(x * x, axis=-1, keepdims=True) + EPS) * g


def _silu(x):
    return x / (1.0 + jnp.exp(-x))


def _softplus(x):
    return jnp.maximum(x, 0.0) + jnp.log(1.0 + jnp.exp(-jnp.abs(x)))


def _iota(shape, dim):
    return lax.broadcasted_iota(I32, shape, dim)


def _head_block_ones(n, dtype=BF16):
    r = _iota((n, n), 0) // HEAD_DIM
    c = _iota((n, n), 1) // HEAD_DIM
    return jnp.where(r == c, 1.0, 0.0).astype(dtype)


def _norm_proj(x, g, w, outs, *, wt=None, precise=False, tm=512):
    m, d = x.shape
    tm = min(tm, m)
    assert m % tm == 0
    n_out = sum(len(o[2]) for o in outs)

    def body(*refs):
        x_ref, g_ref, w_ref = refs[:3]
        pos = 3
        wt_ref = None
        if wt is not None:
            wt_ref = refs[3]
            pos = 4
        o_refs = refs[pos:]
        xn = _rms(x_ref[...], g_ref[...])
        xb = xn if precise else xn.astype(BF16)
        k = 0
        for start, width, variants in outs:
            wslab = w_ref[:, start:start + width]
            r = _dot_hi(xb, wslab) if precise else _dot(xb, wslab)
            for dtype, scale in variants:
                o_refs[k][...] = (r if scale == 1.0 else r * scale).astype(dtype)
                k += 1
        if wt_ref is not None:
            o_refs[k][...] = _dot_nt_hi(wt_ref[...], xb) if precise else _dot_nt(wt_ref[...], xb)

    in_specs = [pl.BlockSpec((tm, d), lambda i: (i, 0)),
                pl.BlockSpec((1, d), lambda i: (0, 0)),
                pl.BlockSpec(w.shape, lambda i: (0, 0))]
    args = [x, g, w]
    if wt is not None:
        in_specs.append(pl.BlockSpec(wt.shape, lambda i: (0, 0)))
        args.append(wt)
    out_shape, out_specs = [], []
    for start, width, variants in outs:
        for dtype, _ in variants:
            out_shape.append(jax.ShapeDtypeStruct((m, width), dtype))
            out_specs.append(pl.BlockSpec((tm, width), lambda i: (i, 0)))
    if wt is not None:
        out_shape.append(jax.ShapeDtypeStruct((wt.shape[0], m), F32))
        out_specs.append(pl.BlockSpec((wt.shape[0], tm), lambda i: (0, i)))
    res = pl.pallas_call(
        body, grid=(m // tm,), in_specs=in_specs, out_specs=out_specs, out_shape=out_shape,
        compiler_params=_cparams("parallel"), name="norm_proj")(*args)
    assert len(res) == n_out + (wt is not None)
    return res


def _resid_proj(res, a_list, w_list, *, precise=False, tm=512, final_g=None):
    m, d = res.shape
    tm = min(tm, m)
    n = len(a_list)

    def body(*refs):
        r_ref = refs[0]
        a_refs = refs[1:1 + n]
        w_refs = refs[1 + n:1 + 2 * n]
        pos = 1 + 2 * n
        acc = r_ref[...]
        for a_ref, w_ref in zip(a_refs, w_refs):
            acc = acc + (_dot_hi(a_ref[...], w_ref[...]) if precise else _dot(a_ref[...], w_ref[...]))
        if final_g is None:
            refs[pos][...] = acc
        else:
            refs[pos + 1][...] = acc
            refs[pos + 2][...] = _rms(acc, refs[pos][...])

    in_specs = [pl.BlockSpec((tm, d), lambda i: (i, 0))]
    in_specs += [pl.BlockSpec((tm, a.shape[1]), lambda i: (i, 0)) for a in a_list]
    in_specs += [pl.BlockSpec(w.shape, lambda i: (0, 0)) for w in w_list]
    args = [res, *a_list, *w_list]
    out_shape = [jax.ShapeDtypeStruct((m, d), F32)]
    out_specs = [pl.BlockSpec((tm, d), lambda i: (i, 0))]
    if final_g is not None:
        in_specs.append(pl.BlockSpec((1, d), lambda i: (0, 0)))
        args.append(final_g)
        out_shape.append(jax.ShapeDtypeStruct((m, d), F32))
        out_specs.append(pl.BlockSpec((tm, d), lambda i: (i, 0)))
    out = pl.pallas_call(
        body, grid=(m // tm,), in_specs=in_specs, out_specs=out_specs, out_shape=out_shape,
        compiler_params=_cparams("parallel"), name="resid_proj")(*args)
    return out[0] if final_g is None else out


def _rope_tables(pos, inv_lane):
    ang = pos * inv_lane
    return jnp.cos(ang), jnp.sin(ang)


def _rotary_lanes(x, cos, sin, first_half):
    w = x.shape[-1]
    reps = w // LANES
    c = jnp.concatenate([cos] * reps, axis=-1) if reps > 1 else cos
    s = jnp.concatenate([sin] * reps, axis=-1) if reps > 1 else sin
    x_up = pltpu.roll(x, w - HEAD_DIM // 2, 1)
    x_dn = pltpu.roll(x, HEAD_DIM // 2, 1)
    return x * c + jnp.where(first_half, -x_up, x_dn) * s


def _retention_prompt(rq, rk, rv, rg, g_ret, inv_lane):
    t, w = rq.shape
    c = math.gcd(t, RET_CHUNK)
    n = t // c
    scale = HEAD_DIM ** -0.5

    def body(q_ref, k_ref, v_ref, gate_ref, gret_ref, inv_ref, o_ref, s_out_ref,
             s_scr, dec_scr, qd_scr, kd_scr):
        i = pl.program_id(0)
        lane = _iota((c, w), 1)
        head_of_lane = lane // HEAD_DIM
        first_half = (lane % HEAD_DIM) < (HEAD_DIM // 2)
        lg_lane = jnp.log(1.0 - jnp.exp2(-5.0 - head_of_lane.astype(F32)))
        row = _iota((c, w), 0).astype(F32)

        @pl.when(i == 0)
        def _():
            s_scr[...] = jnp.zeros_like(s_scr)
            qd_scr[...] = jnp.exp((row + 1.0) * lg_lane)
            kd_scr[...] = jnp.exp((c - 1.0 - row) * lg_lane)
            ri = _iota((c, c), 0)
            ci = _iota((c, c), 1)
            diff = (ri - ci).astype(F32)
            for h in range(N_HEADS):
                lg_h = math.log(1.0 - 2.0 ** (-5.0 - h))
                dec_scr[h] = jnp.where(ri >= ci, jnp.exp(jnp.where(ri >= ci, diff, 0.0) * lg_h), 0.0)

        pos = (i * c).astype(F32) + _iota((c, 1), 0).astype(F32)
        cos, sin = _rope_tables(pos, inv_ref[...])
        q = _rotary_lanes(q_ref[...], cos, sin, first_half)
        k = _rotary_lanes(k_ref[...], cos, sin, first_half) * scale
        qb = q.astype(BF16)
        kb = k.astype(BF16)
        vb = v_ref[...].astype(BF16)
        s_old = s_scr[...]
        o = _dot(qb, s_old.astype(BF16)) * qd_scr[...]
        hms = [head_of_lane == h for h in range(N_HEADS)]
        scs = [_dot_nt(jnp.where(hm, qb, jnp.zeros_like(qb)), kb) for hm in hms]
        ohs = [_dot((scs[h] * dec_scr[h]).astype(BF16), vb) for h in range(N_HEADS)]
        for h in range(N_HEADS):
            o = o + jnp.where(hms[h], ohs[h], 0.0)
        kd = (k * kd_scr[...]).astype(BF16)
        upd = lax.dot_general(kd, vb, (((0,), (0,)), ((), ())), preferred_element_type=F32)
        blk = (_iota((w, w), 0) // HEAD_DIM) == (_iota((w, w), 1) // HEAD_DIM)
        cd = jnp.exp(float(c) * jnp.log(1.0 - jnp.exp2(-5.0 - (_iota((1, w), 1) // HEAD_DIM).astype(F32))))
        s_scr[...] = s_old * cd + jnp.where(blk, upd, 0.0)
        ones = _head_block_ones(w)
        inv_n = 1.0 / HEAD_DIM
        hi, lo = _split2(o)
        mu = (_dot(hi, ones) + _dot(lo, ones)) * inv_n
        dlt = o - mu
        hi, lo = _split2(dlt * dlt)
        var = (_dot(hi, ones) + _dot(lo, ones)) * inv_n
        y = dlt * lax.rsqrt(var + EPS) * gret_ref[...]
        o_ref[...] = (y * _silu(gate_ref[...])).astype(o_ref.dtype)

        @pl.when(i == n - 1)
        def _():
            s_fin = s_scr[...]
            for h in range(N_HEADS):
                s_out_ref[h] = s_fin[h * HEAD_DIM:(h + 1) * HEAD_DIM, h * HEAD_DIM:(h + 1) * HEAD_DIM]

    tok = pl.BlockSpec((c, w), lambda i: (i, 0))
    o, s = pl.pallas_call(
        body, grid=(n,),
        in_specs=[tok, tok, tok, tok, pl.BlockSpec((1, w), lambda i: (0, 0)),
                  pl.BlockSpec((1, LANES), lambda i: (0, 0))],
        out_specs=[tok, pl.BlockSpec((N_HEADS, HEAD_DIM, HEAD_DIM), lambda i: (0, 0, 0))],
        out_shape=[jax.ShapeDtypeStruct((t, w), BF16),
                   jax.ShapeDtypeStruct((N_HEADS, HEAD_DIM, HEAD_DIM), F32)],
        scratch_shapes=[pltpu.VMEM((w, w), F32), pltpu.VMEM((N_HEADS, c, c), F32),
                        pltpu.VMEM((c, w), F32), pltpu.VMEM((c, w), F32)],
        compiler_params=_cparams("arbitrary"), name="retention_prompt")(rq, rk, rv, rg, g_ret, inv_lane)
    return o, s


ATTN_BLOCKS = (512, 256, 128)
SB_CUM_BLOCK = 256
FOX_COL_BLOCK = 256


def _attn_block(t):
    for b in ATTN_BLOCKS:
        if t % b == 0:
            return b
    raise ValueError(f"sequence length {t} must be a multiple of {ATTN_BLOCKS[-1]}")


SOFTPLUS2_LINEAR_ABOVE = 60.0


def _softplus2(z):
    return jnp.where(z > SOFTPLUS2_LINEAR_ABOVE, z, jnp.log(1.0 + jnp.exp2(z)) * LOG2E)


def _sb_prompt(qb, kb, vb):
    t, w = qb.shape
    blk = _attn_block(t)
    cb = min(blk, SB_CUM_BLOCK)
    nblk = blk // cb
    nq = t // blk
    n_pairs = w // LANES

    def body(q_ref, k_ref, v_ref, o_ref, acc_scr, c_scr):
        qi = pl.program_id(1)
        q = q_ref[...]
        lane = _iota((blk, LANES), 1)
        zero = jnp.zeros_like(q)
        q_heads = (jnp.where(lane < HEAD_DIM, q, zero), jnp.where(lane >= HEAD_DIM, q, zero))
        incl = jnp.where(_iota((2 * cb, cb), 0) % cb >= _iota((2 * cb, cb), 1), 1.0, 0.0).astype(BF16)
        acc_scr[...] = jnp.zeros_like(acc_scr)
        c_scr[...] = jnp.zeros_like(c_scr)

        heads = range(2)

        def blocks(start, strict):
            order = [(b, hh) for b in reversed(range(nblk)) for hh in heads]
            ks = {b: k_ref[pl.ds(start + b * cb, cb), :] for b in range(nblk)}
            vs = {b: v_ref[pl.ds(start + b * cb, cb), :] for b in range(nblk)}
            masks = {b: None if strict is None else strict[:, b * cb:(b + 1) * cb] for b in range(nblk)}
            zs = {c: _dot_nt(q_heads[c[1]], ks[c[0]]) for c in order}
            sps = {c: _softplus2(zs[c]) for c in order}
            if strict is not None:
                sps = {c: jnp.where(masks[c[0]], sps[c], 0.0) for c in order}
            laters = {c: _dot(sps[c].astype(BF16), incl[:cb]) for c in order}
            carries = [c_scr[hh] for hh in heads]
            wgts = {}
            for c in order:
                b, hh = c
                wgt = jnp.exp2(zs[c] - laters[c] - carries[hh])
                wgts[c] = (wgt if strict is None else jnp.where(masks[b], wgt, 0.0)).astype(BF16)
                carries[hh] = carries[hh] + jnp.sum(sps[c], axis=1, keepdims=True)
            outs = {c: _dot(wgts[c], vs[c[0]]) for c in order}
            for hh in heads:
                tot = outs[(0, hh)]
                for b in range(1, nblk):
                    tot = tot + outs[(b, hh)]
                acc_scr[hh] += tot
                c_scr[hh] = carries[hh]

        blocks(pl.multiple_of(qi * blk, blk), _iota((blk, blk), 1) < _iota((blk, blk), 0))

        def step(j, _):
            blocks(pl.multiple_of((qi - 1 - j) * blk, blk), None)
            return 0

        lax.fori_loop(0, qi, step, 0)
        o_ref[...] = jnp.where(lane < HEAD_DIM, acc_scr[0], acc_scr[1]).astype(o_ref.dtype)

    return pl.pallas_call(
        body, grid=(n_pairs, nq),
        in_specs=[pl.BlockSpec((blk, LANES), lambda p, i: (i, p)),
                  pl.BlockSpec((t, LANES), lambda p, i: (0, p)),
                  pl.BlockSpec((t, LANES), lambda p, i: (0, p))],
        out_specs=pl.BlockSpec((blk, LANES), lambda p, i: (i, p)),
        out_shape=jax.ShapeDtypeStruct((t, w), BF16),
        scratch_shapes=[pltpu.VMEM((2, blk, LANES), F32), pltpu.VMEM((2, blk, 1), F32)],
        compiler_params=_cparams("parallel", "arbitrary"), name="sb_prompt")(qb, kb, vb)


def _fox_prompt(qb, kb, vb, cum_row):
    t, w = qb.shape
    blk = _attn_block(t)
    cb = min(blk, FOX_COL_BLOCK)
    n_cb = blk // cb
    nq = t // blk
    n_pairs = w // LANES

    def body(q_ref, k_ref, v_ref, ck_ref, o_ref, acc_scr, m_scr, z_scr):
        p = pl.program_id(0)
        qi = pl.program_id(1)
        q = q_ref[...]
        lane = _iota((blk, LANES), 1)
        first = lane < HEAD_DIM
        zero = jnp.zeros_like(q)
        q_heads = (jnp.where(first, q, zero), jnp.where(first, zero, q))
        acc_scr[...] = jnp.zeros_like(acc_scr)
        m_scr[...] = jnp.full_like(m_scr, -jnp.inf)
        rsel = _iota((N_HEADS, blk), 0)
        heads = range(2)

        def logits(start):
            k = k_ref[pl.ds(start, blk), :]
            return [_dot_nt(q_heads[hh], k) for hh in heads]

        def blocks(start, causal, zs, next_start, next_slot):
            z_next = logits(next_start)
            for hh in heads:
                z_scr[next_slot, hh] = z_next[hh]
            v = v_ref[pl.ds(start, blk), :]
            ck_all = ck_ref[:, pl.ds(start, blk)]
            one = jnp.ones_like(v)
            v_aug = (jnp.where(first, v, one), jnp.where(first, one, v))
            ss = []
            for hh in heads:
                ck = jnp.sum(jnp.where(rsel == 2 * p + hh, ck_all, 0.0), axis=0, keepdims=True)
                s = zs[hh] - ck
                ss.append(s if causal is None else jnp.where(causal, s, -jnp.inf))
            maxes = [jnp.max(s, axis=1, keepdims=True) for s in ss]
            m_olds = [m_scr[hh] for hh in heads]
            m_news = [jnp.maximum(m_olds[hh], maxes[hh]) for hh in heads]
            alphas = [jnp.exp2(m_olds[hh] - m_news[hh]) for hh in heads]
            order = [(b, hh) for b in range(n_cb) for hh in heads]
            prs = {(b, hh): jnp.exp2(ss[hh][:, b * cb:(b + 1) * cb] - m_news[hh]).astype(BF16) for b, hh in order}
            outs = {(b, hh): _dot(prs[(b, hh)], v_aug[hh][b * cb:(b + 1) * cb]) for b, hh in order}
            for hh in heads:
                tot = outs[(0, hh)]
                for b in range(1, n_cb):
                    tot = tot + outs[(b, hh)]
                acc_scr[hh] = alphas[hh] * acc_scr[hh] + tot
                m_scr[hh] = m_news[hh]

        def start_of(block):
            return pl.multiple_of(jnp.maximum(block, 0) * blk, blk)

        blocks(start_of(qi), _iota((blk, blk), 1) <= _iota((blk, blk), 0), logits(start_of(qi)), start_of(qi - 1), 0)

        def step(j, _):
            slot = j % 2
            blocks(start_of(qi - 1 - j), None, [z_scr[slot, hh] for hh in heads], start_of(qi - 2 - j), 1 - slot)
            return 0

        lax.fori_loop(0, qi, step, 0)
        outs = [acc_scr[hh] / pltpu.roll(acc_scr[hh], HEAD_DIM, 1) for hh in heads]
        o_ref[...] = jnp.where(first, outs[0], outs[1]).astype(o_ref.dtype)

    return pl.pallas_call(
        body, grid=(n_pairs, nq),
        in_specs=[pl.BlockSpec((blk, LANES), lambda p, i: (i, p)),
                  pl.BlockSpec((t, LANES), lambda p, i: (0, p)),
                  pl.BlockSpec((t, LANES), lambda p, i: (0, p)),
                  pl.BlockSpec((N_HEADS, t), lambda p, i: (0, 0))],
        out_specs=pl.BlockSpec((blk, LANES), lambda p, i: (i, p)),
        out_shape=jax.ShapeDtypeStruct((t, w), BF16),
        scratch_shapes=[pltpu.VMEM((2, blk, LANES), F32), pltpu.VMEM((2, blk, 1), F32),
                        pltpu.VMEM((2, 2, blk, blk), F32)],
        compiler_params=_cparams("parallel", "arbitrary"), name="fox_prompt")(qb, kb, vb, cum_row)


def _gate_math(x, bias, a_log, idx):
    xb = x + bias
    e = jnp.exp(-jnp.abs(xb))
    l1p = jnp.log(1.0 + e)
    logf = jnp.minimum(xb, 0.0) - l1p
    g = -jnp.exp(a_log) * (jnp.maximum(xb, 0.0) + l1p)
    beta = 1.0 / (1.0 + jnp.exp(-xb))
    h = N_HEADS
    return jnp.where(idx < h, logf, jnp.where(idx < 2 * h, g, jnp.where(idx < 3 * h, beta, 0.0)))


def _gates(small_col, small_row, bias_lane, alog_lane, bias_col, alog_col, *, tm=2048):
    m = small_col.shape[0]
    r = small_row.shape[0]
    tm = min(tm, m)

    def body(c_ref, r_ref, bl_ref, al_ref, bc_ref, ac_ref, oc_ref, or_ref):
        oc_ref[...] = _gate_math(c_ref[...], bl_ref[...], al_ref[...], _iota((tm, LANES), 1))
        or_ref[...] = _gate_math(r_ref[...], bc_ref[...], ac_ref[...], _iota((r, tm), 0))

    return pl.pallas_call(
        body, grid=(m // tm,),
        in_specs=[pl.BlockSpec((tm, LANES), lambda i: (i, 0)), pl.BlockSpec((r, tm), lambda i: (0, i)),
                  pl.BlockSpec((1, LANES), lambda i: (0, 0)), pl.BlockSpec((1, LANES), lambda i: (0, 0)),
                  pl.BlockSpec((r, 1), lambda i: (0, 0)), pl.BlockSpec((r, 1), lambda i: (0, 0))],
        out_specs=[pl.BlockSpec((tm, LANES), lambda i: (i, 0)), pl.BlockSpec((r, tm), lambda i: (0, i))],
        out_shape=[jax.ShapeDtypeStruct((m, LANES), F32), jax.ShapeDtypeStruct((r, m), F32)],
        compiler_params=_cparams("parallel"), name="gates")(
            small_col, small_row, bias_lane, alog_lane, bias_col, alog_col)


def _cum_forget(logf_rows, n_chunks):
    r = logf_rows.shape[0]

    def body(x_ref, o_ref):
        x = x_ref[...]
        incl = jnp.where(_iota((LANES, LANES), 0) <= _iota((LANES, LANES), 1), 1.0, 0.0).astype(BF16)
        local = _dot_x3(x, incl)
        tot = jnp.broadcast_to(local[:, LANES - 1:LANES], (r, LANES))
        ri = _iota((r, r), 0)
        ci = _iota((r, r), 1)
        before = jnp.where((ri // n_chunks == ci // n_chunks) & (ci < ri), 1.0, 0.0).astype(BF16)
        o_ref[...] = (local + _dot_3x(before, tot)) * LOG2E

    return pl.pallas_call(
        body, grid=(1,),
        in_specs=[pl.BlockSpec((r, LANES), lambda i: (0, 0))],
        out_specs=pl.BlockSpec((r, LANES), lambda i: (0, 0)),
        out_shape=jax.ShapeDtypeStruct((r, LANES), F32),
        compiler_params=_cparams("arbitrary"), name="cum_forget")(logf_rows)


def _gdn_prep(dqkv, conv_w, *, tm=256):
    t, w3 = dqkv.shape
    w = w3 // 3
    tm = min(tm, t)
    scale = HEAD_DIM ** -0.5

    def body(x_ref, prev_ref, cw_ref, q_ref, k_ref, kt_ref, v_ref):
        i = pl.program_id(0)
        x = x_ref[...]
        prev = jnp.where(i == 0, 0.0, prev_ref[...])
        xp = jnp.concatenate([prev, x], axis=0)
        cw = cw_ref[...]
        off = SUBLANES - (CONV_W - 1)
        y = xp[off:off + tm] * cw[0:1]
        for j in range(1, CONV_W):
            y = y + xp[off + j:off + j + tm] * cw[j:j + 1]
        y = _silu(y)
        ones = _head_block_ones(w)

        def l2n(a):
            hi, lo = _split2(a * a)
            return a * lax.rsqrt(_dot(hi, ones) + _dot(lo, ones) + EPS)

        qn = l2n(y[:, :w]) * scale
        kn = l2n(y[:, w:2 * w])
        vv = y[:, 2 * w:]
        for h in range(N_HEADS):
            sl = slice(h * HEAD_DIM, (h + 1) * HEAD_DIM)
            q_ref[h] = qn[:, sl]
            k_ref[h] = kn[:, sl]
            v_ref[h] = vv[:, sl]
        kt_ref[...] = kn.T

    hm = pl.BlockSpec((N_HEADS, tm, HEAD_DIM), lambda i: (0, i, 0))
    hm_shape = jax.ShapeDtypeStruct((N_HEADS, t, HEAD_DIM), F32)
    return pl.pallas_call(
        body, grid=(t // tm,),
        in_specs=[pl.BlockSpec((tm, w3), lambda i: (i, 0)),
                  pl.BlockSpec((SUBLANES, w3), lambda i: (jnp.maximum(i * (tm // SUBLANES) - 1, 0), 0)),
                  pl.BlockSpec((CONV_W, w3), lambda i: (0, 0))],
        out_specs=[hm, hm, pl.BlockSpec((w, tm), lambda i: (0, i)), hm],
        out_shape=[hm_shape, hm_shape, jax.ShapeDtypeStruct((w, t), F32), hm_shape],
        compiler_params=_cparams("parallel"), name="gdn_prep")(dqkv, dqkv, conv_w)


def _mm3(a, b):
    ah, al = _split2(a)
    bh, bl = _split2(b)
    return _dot(ah, bh) + _dot(ah, bl) + _dot(al, bh)


def _gdn_wy(q, k, kt, v, post_col, post_row):
    nh, t, d = q.shape
    c = math.gcd(t, GDN_CHUNK)
    span = LANES if t % LANES == 0 else t
    per = span // c

    def body(q_ref, k_ref, kt_ref, v_ref, pc_ref, pr_ref, u_ref, w_ref, qe_ref, qk_ref, kdt_ref, egl_ref):
        ri = _iota((c, c), 0)
        ci = _iota((c, c), 1)
        tril = ri >= ci
        strict = ri > ci
        lower_incl = jnp.where(tril, 1.0, 0.0).astype(BF16)
        upper_incl = jnp.where(ri <= ci, 1.0, 0.0).astype(BF16)
        xs, ys = [], []
        units = [(s, h) for s in range(per) for h in range(nh)]
        cums = []
        for s in range(per):
            rows = slice(s * c, (s + 1) * c)
            cums.append((_dot_3x(lower_incl, pc_ref[rows, :]),
                         _dot_x3(pr_ref[:, rows], upper_incl)))
        for s, h in units:
            rows = slice(s * c, (s + 1) * c)
            cg_col_all, cg_row_all = cums[s]
            cgc = cg_col_all[:, nh + h:nh + h + 1]
            cgr = cg_row_all[nh + h:nh + h + 1, :]
            beta = pc_ref[rows, 2 * nh + h:2 * nh + h + 1]
            g_last = cgr[:, c - 1:c]
            decay = jnp.where(tril, jnp.exp(jnp.where(tril, cgc - cgr, 0.0)), 0.0)
            qh = q_ref[h, rows, :]
            kth = kt_ref[h * d:(h + 1) * d, rows]
            k_beta = k_ref[h, rows, :] * beta
            gram = _dot(jnp.concatenate([k_beta, qh], axis=0).astype(BF16), kth.astype(BF16))
            qk_ref[h, rows, :] = jnp.where(tril, gram[c:] * decay, 0.0).astype(BF16)
            qe_ref[h, rows, :] = (qh * jnp.exp(cgc)).astype(BF16)
            kdt_ref[h * d:(h + 1) * d, rows] = (kth * jnp.exp(g_last - cgr)).astype(BF16)
            egl_ref[h, rows, :] = jnp.broadcast_to(jnp.exp(g_last), (c, d))
            xs.append(jnp.where(strict, gram[:c] * decay, 0.0))
            ys.append(jnp.concatenate([v_ref[h, rows, :] * beta, k_beta * jnp.exp(cgc)], axis=1))
        base = GDN_INV_BASE
        assert c % base == 0 and (c // base) & (c // base - 1) == 0
        eye = jnp.where(ri == ci, 1.0, 0.0)
        n_u = len(units)

        def same_block(size):
            return (ri // size) == (ci // size)

        ts = [eye - jnp.where(same_block(base), a, 0.0) for a in xs]
        pws = [t - eye for t in ts]
        span_done = 2
        while span_done < base:
            sq = [_mm3(pws[i], pws[i]) for i in range(n_u)]
            ts = [ts[i] + _mm3(ts[i], sq[i]) for i in range(n_u)]
            pws = sq
            span_done *= 2
        size = base
        while 2 * size < c:
            coup = same_block(2 * size) & ~same_block(size)
            prods = [_mm3(jnp.where(coup, xs[i], 0.0), ts[i]) for i in range(n_u)]
            ts = [ts[i] - _mm3(ts[i], prods[i]) for i in range(n_u)]
            size *= 2
        if size < c:
            coup = ~same_block(size)
            tys = [_mm3(ts[i], ys[i]) for i in range(n_u)]
            prods = [_mm3(jnp.where(coup, xs[i], 0.0), tys[i]) for i in range(n_u)]
            ys = [tys[i] - _mm3(ts[i], prods[i]) for i in range(n_u)]
        else:
            ys = [_mm3(ts[i], ys[i]) for i in range(n_u)]
        for i, (s, h) in enumerate(units):
            rows = slice(s * c, (s + 1) * c)
            u_ref[h, rows, :] = ys[i][:, :d]
            w_ref[h, rows, :] = ys[i][:, d:].astype(BF16)

    hm = pl.BlockSpec((nh, span, d), lambda i: (0, i, 0))
    ktb = pl.BlockSpec((nh * d, span), lambda i: (0, i))
    return pl.pallas_call(
        body, grid=(t // span,),
        in_specs=[hm, hm, ktb, hm, pl.BlockSpec((span, LANES), lambda i: (i, 0)),
                  pl.BlockSpec((post_row.shape[0], span), lambda i: (0, i))],
        out_specs=[hm, hm, hm, hm, ktb, hm],
        out_shape=[jax.ShapeDtypeStruct((nh, t, d), F32), jax.ShapeDtypeStruct((nh, t, d), BF16),
                   jax.ShapeDtypeStruct((nh, t, d), BF16), jax.ShapeDtypeStruct((nh, t, d), BF16),
                   jax.ShapeDtypeStruct((nh * d, t), BF16), jax.ShapeDtypeStruct((nh, t, d), F32)],
        compiler_params=_cparams("parallel"), name="gdn_wy")(q, k, kt, v, post_col, post_row)


def _gdn_scan(u, w, qe, qk, kdt, egl, dz, g_gdn):
    nh, t, d = u.shape
    c = math.gcd(t, GDN_CHUNK)
    span = LANES if t % LANES == 0 else t
    per = span // c
    n = t // span

    def body(u_ref, w_ref, qe_ref, qk_ref, kdt_ref, egl_ref, dz_ref, gg_ref, o_ref, s_out_ref, s_scr, o_scr):
        i = pl.program_id(0)

        @pl.when(i == 0)
        def _():
            s_scr[...] = jnp.zeros_like(s_scr)

        heads = range(nh)
        states = [s_scr[h] for h in heads]
        for j in range(per):
            rows = slice(j * c, (j + 1) * c)
            sbs = [states[h].astype(BF16) for h in heads]
            ws = [_dot(w_ref[h, rows, :], sbs[h]) for h in heads]
            qs = [_dot(qe_ref[h, rows, :], sbs[h]) for h in heads]
            vbs = [(u_ref[h, rows, :] - ws[h]).astype(BF16) for h in heads]
            os_ = [qs[h] + _dot(qk_ref[h, rows, :], vbs[h]) for h in heads]
            upd = [_dot(kdt_ref[h * d:(h + 1) * d, rows], vbs[h]) for h in heads]
            for h in heads:
                states[h] = states[h] * egl_ref[h, rows, :] + upd[h]
                o = os_[h]
                y = o * lax.rsqrt(jnp.mean(o * o, axis=-1, keepdims=True) + EPS) * gg_ref[...]
                o_scr[rows, h * d:(h + 1) * d] = y
        for h in heads:
            s_scr[h] = states[h]
        o_ref[...] = (o_scr[...] * _silu(dz_ref[...])).astype(o_ref.dtype)

        @pl.when(i == n - 1)
        def _():
            s_out_ref[...] = s_scr[...]

    hm = pl.BlockSpec((nh, span, d), lambda i: (0, i, 0))
    return pl.pallas_call(
        body, grid=(n,),
        in_specs=[hm, hm, hm, hm, pl.BlockSpec((nh * d, span), lambda i: (0, i)), hm,
                  pl.BlockSpec((span, nh * d), lambda i: (i, 0)), pl.BlockSpec((1, d), lambda i: (0, 0))],
        out_specs=[pl.BlockSpec((span, nh * d), lambda i: (i, 0)),
                   pl.BlockSpec((nh, d, d), lambda i: (0, 0, 0))],
        out_shape=[jax.ShapeDtypeStruct((t, nh * d), BF16), jax.ShapeDtypeStruct((nh, d, d), F32)],
        scratch_shapes=[pltpu.VMEM((nh, d, d), F32), pltpu.VMEM((span, nh * d), F32)],
        compiler_params=_cparams("arbitrary"), name="gdn_scan")(u, w, qe, qk, kdt, egl, dz, g_gdn)


def _mem_attn_core(q, mk, mv, precise):
    scale = MEM_HEAD_DIM ** -0.5
    outs = []
    for h in range(MEM_HEADS):
        sl = slice(h * MEM_HEAD_DIM, (h + 1) * MEM_HEAD_DIM)
        if precise:
            s = _dot_nt_hi(q[:, sl], mk[:, sl]) * scale
        else:
            s = _dot_nt(q[:, sl].astype(BF16), mk[:, sl]) * scale
        e = jnp.exp(s - jnp.max(s, axis=-1, keepdims=True))
        p = e / jnp.sum(e, axis=-1, keepdims=True)
        outs.append(_dot_hi(p, mv[:, sl]) if precise else _dot(p.astype(BF16), mv[:, sl]))
    return jnp.concatenate(outs, axis=-1)


def _mem_attn_prompt(y, g, wq, mk, mv, wo, *, tm=512):
    m, d = y.shape
    tm = min(tm, m)

    def body(y_ref, g_ref, wq_ref, mk_ref, mv_ref, wo_ref, o_ref):
        yv = y_ref[...]
        hb = _rms(yv, g_ref[...]).astype(BF16)
        q = _dot(hb, wq_ref[...])
        o = _mem_attn_core(q, mk_ref[...], mv_ref[...], False)
        o_ref[...] = yv + _dot(o.astype(BF16), wo_ref[...])

    full = lambda a: pl.BlockSpec(a.shape, lambda i: (0, 0))
    return pl.pallas_call(
        body, grid=(m // tm,),
        in_specs=[pl.BlockSpec((tm, d), lambda i: (i, 0)), pl.BlockSpec((1, d), lambda i: (0, 0)),
                  full(wq), full(mk), full(mv), full(wo)],
        out_specs=pl.BlockSpec((tm, d), lambda i: (i, 0)),
        out_shape=jax.ShapeDtypeStruct((m, d), F32),
        compiler_params=_cparams("parallel"), name="mem_attn_prompt")(y, g, wq, mk, mv, wo)


def _mem_attn_sample(q, mem_k, mem_v):
    b, _, w = q.shape
    n_mem = mem_k.shape[1]

    def body(q_ref, k_ref, v_ref, o_ref):
        q8 = jnp.broadcast_to(q_ref[0], (SUBLANES, w))
        o = _mem_attn_core(q8, k_ref[0], v_ref[0], True)
        o_ref[0] = o[0:1]

    return pl.pallas_call(
        body, grid=(b,),
        in_specs=[pl.BlockSpec((1, 1, w), lambda i: (i, 0, 0)),
                  pl.BlockSpec((1, n_mem, w), lambda i: (i, 0, 0)),
                  pl.BlockSpec((1, n_mem, w), lambda i: (i, 0, 0))],
        out_specs=pl.BlockSpec((1, 1, w), lambda i: (i, 0, 0)),
        out_shape=jax.ShapeDtypeStruct((b, 1, w), F32),
        compiler_params=_cparams("parallel"), name="mem_attn_sample")(q, mem_k, mem_v)


def _route(logits):
    m = logits.shape[0]
    lane = _iota((m, LANES), 1)
    big = LANES
    neg = -jnp.inf
    is_g = lane < N_GROUPS
    gl = jnp.where(is_g, logits, neg)
    gmax = jnp.max(gl, axis=1, keepdims=True)
    g_val = 1.0 / jnp.sum(jnp.where(is_g, jnp.exp(gl - gmax), 0.0), axis=1, keepdims=True)
    g_idx = jnp.min(jnp.where(is_g & (gl == gmax), lane, big), axis=1, keepdims=True)
    lo = N_GROUPS + g_idx * EXPERTS_PER_GROUP
    in_grp = (lane >= lo) & (lane < lo + EXPERTS_PER_GROUP)
    el = jnp.where(in_grp, logits, neg)
    l1 = jnp.max(el, axis=1, keepdims=True)
    i1 = jnp.min(jnp.where(in_grp & (el == l1), lane, big), axis=1, keepdims=True)
    el2 = jnp.where(lane == i1, neg, el)
    l2 = jnp.max(el2, axis=1, keepdims=True)
    i2 = jnp.min(jnp.where(in_grp & (lane != i1) & (el2 == l2), lane, big), axis=1, keepdims=True)
    p2 = jnp.exp(l2 - l1)
    w1 = g_val / (1.0 + p2)
    w2 = g_val * p2 / (1.0 + p2)
    return jnp.where(lane == i1, w1, jnp.where(lane == i2, w2, 0.0))


def _moe_dense(y, g, w_route, w_gate, w_up, w_down, *, precise=False, tm=1024, final_g=None):
    m, d = y.shape
    tm = min(tm, m)
    ne, _, ff = w_gate.shape
    xdt = F32 if precise else BF16

    def body(*refs):
        y_ref, g_ref, wr_ref, wg_ref, wu_ref, wd_ref = refs[:6]
        pos = 6
        fg_ref = None
        if final_g is not None:
            fg_ref = refs[6]
            pos = 7
        o_ref = refs[pos]
        pos += 1
        on_ref = None
        if final_g is not None:
            on_ref = refs[pos]
            pos += 1
        xn_scr, gate_scr, acc_scr = refs[pos:pos + 3]
        e = pl.program_id(1)

        @pl.when(e == 0)
        def _():
            xn = _rms(y_ref[...], g_ref[...])
            xn_scr[...] = xn.astype(xdt)
            logits = _dot_hi(xn, wr_ref[...]) if precise else _dot(xn.astype(BF16), wr_ref[...])
            gate_scr[...] = _route(logits)
            acc_scr[...] = jnp.zeros_like(acc_scr)

        xn = xn_scr[...]
        if precise:
            hg = _dot_hi(xn, wg_ref[0])
            hu = _dot_hi(xn, wu_ref[0])
            out = _dot_hi(_silu(hg) * hu, wd_ref[0])
        else:
            hg = _dot(xn, wg_ref[0])
            hu = _dot(xn, wu_ref[0])
            out = _dot((_silu(hg) * hu).astype(BF16), wd_ref[0])
        lane = _iota((tm, LANES), 1)
        ge = jnp.sum(jnp.where(lane == N_GROUPS + e, gate_scr[...], 0.0), axis=1, keepdims=True)
        acc_scr[...] += ge * out

        @pl.when(e == ne - 1)
        def _():
            res = y_ref[...] + acc_scr[...]
            o_ref[...] = res
            if on_ref is not None:
                on_ref[...] = _rms(res, fg_ref[...])

    tok = pl.BlockSpec((tm, d), lambda i, e: (i, 0))
    in_specs = [tok, pl.BlockSpec((1, d), lambda i, e: (0, 0)), pl.BlockSpec(w_route.shape, lambda i, e: (0, 0)),
                pl.BlockSpec((1, d, ff), lambda i, e: (e, 0, 0)), pl.BlockSpec((1, d, ff), lambda i, e: (e, 0, 0)),
                pl.BlockSpec((1, ff, d), lambda i, e: (e, 0, 0))]
    args = [y, g, w_route, w_gate, w_up, w_down]
    out_shape = [jax.ShapeDtypeStruct((m, d), F32)]
    out_specs = [tok]
    if final_g is not None:
        in_specs.append(pl.BlockSpec((1, d), lambda i, e: (0, 0)))
        args.append(final_g)
        out_shape.append(jax.ShapeDtypeStruct((m, d), F32))
        out_specs.append(tok)
    res = pl.pallas_call(
        body, grid=(m // tm, ne), in_specs=in_specs, out_specs=out_specs, out_shape=out_shape,
        scratch_shapes=[pltpu.VMEM((tm, d), xdt), pltpu.VMEM((tm, LANES), F32), pltpu.VMEM((tm, d), F32)],
        compiler_params=_cparams("parallel", "arbitrary"), name="moe_dense")(*args)
    return res[0] if final_g is None else res


def _heads3(x):
    return x.reshape(N_HEADS, HEAD_DIM, x.shape[-1])


def _retention_sample(q_col, k_col, v, gate, state, g_ret, inv_col, pos):
    b, w, _ = q_col.shape
    scale = HEAD_DIM ** -0.5
    half = HEAD_DIM // 2

    def body(q_ref, k_ref, v_ref, gate_ref, s_ref, gr_ref, inv_ref, o_ref, so_ref):
        row = _iota((w, 1), 0)
        first_half = (row % HEAD_DIM) < half
        ang = float(pos) * inv_ref[...]
        cos, sin = jnp.cos(ang), jnp.sin(ang)

        def rot(x):
            return x * cos + jnp.where(first_half, -pltpu.roll(x, w - half, 0), pltpu.roll(x, half, 0)) * sin

        q = rot(q_ref[0])
        k = rot(k_ref[0]) * scale
        gamma = 1.0 - jnp.exp2(-5.0 - (row // HEAD_DIM).astype(F32))
        s = s_ref[0]
        vv = v_ref[0]
        qk = jnp.sum(_heads3(q * k), axis=1)
        qs = jnp.sum(_heads3(q * s * gamma), axis=1)
        o = qk * vv + qs
        v_rows = jnp.broadcast_to(vv[:, None, :], (N_HEADS, HEAD_DIM, HEAD_DIM)).reshape(w, HEAD_DIM)
        so_ref[0] = s * gamma + k * v_rows
        mu = jnp.mean(o, axis=-1, keepdims=True)
        dlt = o - mu
        var = jnp.mean(dlt * dlt, axis=-1, keepdims=True)
        o_ref[0] = dlt * lax.rsqrt(var + EPS) * gr_ref[...] * _silu(gate_ref[0])

    col = pl.BlockSpec((1, w, 1), lambda i: (i, 0, 0))
    hd = pl.BlockSpec((1, N_HEADS, HEAD_DIM), lambda i: (i, 0, 0))
    st = pl.BlockSpec((1, w, HEAD_DIM), lambda i: (i, 0, 0))
    return pl.pallas_call(
        body, grid=(b,),
        in_specs=[col, col, hd, hd, st, pl.BlockSpec((N_HEADS, HEAD_DIM), lambda i: (0, 0)),
                  pl.BlockSpec((w, 1), lambda i: (0, 0))],
        out_specs=[hd, st],
        out_shape=[jax.ShapeDtypeStruct((b, N_HEADS, HEAD_DIM), F32), jax.ShapeDtypeStruct((b, w, HEAD_DIM), F32)],
        compiler_params=_cparams("parallel"), name="retention_sample")(q_col, k_col, v, gate, state, g_ret, inv_col)


def _gdn_sample(xqk_col, bufqk_col, cwqk_col, xv, bufv, cwv, dz, small_col, bias_col, alog_col, state, g_gdn):
    b = xqk_col.shape[0]
    w = N_HEADS * HEAD_DIM
    scale = HEAD_DIM ** -0.5
    r = small_col.shape[1]

    def body(x_ref, bq_ref, cq_ref, xv_ref, bv_ref, cv_ref, dz_ref, sm_ref, bc_ref, ac_ref, s_ref, gg_ref,
             o_ref, so_ref):
        yqk = x_ref[0] * cq_ref[CONV_W - 1]
        yv = xv_ref[0] * cv_ref[CONV_W - 1]
        for j in range(CONV_W - 1):
            yqk = yqk + bq_ref[0, j] * cq_ref[j]
            yv = yv + bv_ref[0, j] * cv_ref[j]
        yqk = _silu(yqk)
        vv = _silu(yv)

        def l2n(a):
            a3 = _heads3(a)
            ss = jnp.sum(a3 * a3, axis=1, keepdims=True)
            return a3 * lax.rsqrt(ss + EPS)

        q3 = l2n(yqk[:w]) * scale
        k3 = l2n(yqk[w:])
        post = _gate_math(sm_ref[0], bc_ref[...], ac_ref[...], _iota((r, 1), 0))
        g = post[N_HEADS:2 * N_HEADS]
        beta = post[2 * N_HEADS:3 * N_HEADS]
        a = jnp.exp(g)
        s3 = _heads3(s_ref[0])
        ks = jnp.sum(k3 * s3, axis=1)
        v_new = beta * vv - (beta * a) * ks
        o = a * jnp.sum(q3 * s3, axis=1) + jnp.sum(q3 * k3, axis=1) * v_new
        s_new = s3 * a[:, :, None] + k3 * v_new[:, None, :]
        so_ref[0] = s_new.reshape(w, HEAD_DIM)
        y = o * lax.rsqrt(jnp.mean(o * o, axis=-1, keepdims=True) + EPS) * gg_ref[...]
        o_ref[0] = y * _silu(dz_ref[0])

    hd = pl.BlockSpec((1, N_HEADS, HEAD_DIM), lambda i: (i, 0, 0))
    st = pl.BlockSpec((1, w, HEAD_DIM), lambda i: (i, 0, 0))
    return pl.pallas_call(
        body, grid=(b,),
        in_specs=[pl.BlockSpec((1, 2 * w, 1), lambda i: (i, 0, 0)),
                  pl.BlockSpec((1, CONV_W - 1, 2 * w, 1), lambda i: (i, 0, 0, 0)),
                  pl.BlockSpec((CONV_W, 2 * w, 1), lambda i: (0, 0, 0)),
                  hd, pl.BlockSpec((1, CONV_W - 1, N_HEADS, HEAD_DIM), lambda i: (i, 0, 0, 0)),
                  pl.BlockSpec((CONV_W, N_HEADS, HEAD_DIM), lambda i: (0, 0, 0)),
                  hd, pl.BlockSpec((1, r, 1), lambda i: (i, 0, 0)),
                  pl.BlockSpec((r, 1), lambda i: (0, 0)), pl.BlockSpec((r, 1), lambda i: (0, 0)),
                  st, pl.BlockSpec((1, HEAD_DIM), lambda i: (0, 0))],
        out_specs=[hd, st],
        out_shape=[jax.ShapeDtypeStruct((b, N_HEADS, HEAD_DIM), F32), jax.ShapeDtypeStruct((b, w, HEAD_DIM), F32)],
        compiler_params=_cparams("parallel"), name="gdn_sample")(
            xqk_col, bufqk_col, cwqk_col, xv, bufv, cwv, dz, small_col, bias_col, alog_col, state, g_gdn)


PAGES_PER_STEP = 16


def _paged_sweep(mode, q_col, cache_kt, cache_vt, page_table, page_base, *, k_new=None, v_new=None,
                 cache_lft=None, logf_new=None):
    fox = mode == "fox"
    b, nh, hd, _ = q_col.shape
    n_pages = page_table.shape[1]
    page = cache_kt.shape[-1]
    pps = math.gcd(n_pages, PAGES_PER_STEP)
    ng = n_pages // pps
    qscale = hd ** -0.5 * LOG2E
    n_split = 3 if fox else 2

    def body(*refs):
        q_ref = refs[1]
        pos = 2
        k_refs = refs[pos:pos + pps]
        pos += pps
        v_refs = refs[pos:pos + pps]
        pos += pps
        if fox:
            lf_refs = refs[pos:pos + pps]
            pos += pps
            kn_ref, vn_ref, lfn_ref = refs[pos:pos + 3]
            pos += 3
        o_ref = refs[pos]
        pos += 1
        acc_scr, carry_scr = refs[pos:pos + 2]
        pos += 2
        if fox:
            m_scr, l_scr = refs[pos:pos + 2]
        j = pl.program_id(1)
        q3 = q_ref[0] * qscale
        li = _iota((page, page), 0)
        lj = _iota((page, page), 1)
        later = jnp.where((li > lj) if fox else (li >= lj), 1.0, 0.0).astype(BF16)
        later = jnp.concatenate([later] * n_split, axis=0)

        @pl.when(j == 0)
        def _():
            if fox:
                m_scr[...] = jnp.sum(q3 * kn_ref[0], axis=1)
                l_scr[...] = jnp.ones_like(l_scr)
                acc_scr[...] = jnp.where(_iota((nh, hd, page), 2) == 0, vn_ref[0], 0.0)
                carry_scr[...] = lfn_ref[0] * LOG2E
            else:
                acc_scr[...] = jnp.zeros_like(acc_scr)
                carry_scr[...] = jnp.zeros_like(carry_scr)

        pages = range(pps)
        zs = [jnp.sum(k_refs[p][0] * q3, axis=1) for p in pages]
        carry = carry_scr[...]
        if not fox:
            sps = [_softplus2(z) for z in zs]
            laters = [_dot(jnp.concatenate(_split2(sp), axis=1), later) for sp in sps]
            wts = []
            for p in pages:
                wts.append(jnp.exp2(zs[p] - laters[p] - carry))
                carry = carry + jnp.sum(sps[p], axis=1, keepdims=True)
            acc = acc_scr[...]
        else:
            lfs = [lf_refs[p][0] * LOG2E for p in pages]
            laters = [_dot(jnp.concatenate(_split3(lf), axis=1), later) for lf in lfs]
            logits = []
            for p in pages:
                logits.append(zs[p] + (laters[p] + carry))
                carry = carry + jnp.sum(lfs[p], axis=1, keepdims=True)
            m_old = m_scr[...]
            m_new = m_old
            for s in logits:
                m_new = jnp.maximum(m_new, jnp.max(s, axis=1, keepdims=True))
            alpha = jnp.exp2(m_old - m_new)
            wts = [jnp.exp2(s - m_new) for s in logits]
            l_new = alpha * l_scr[...]
            for wt in wts:
                l_new = l_new + jnp.sum(wt, axis=1, keepdims=True)
            l_scr[...] = l_new
            m_scr[...] = m_new
            acc = acc_scr[...] * alpha[:, :, None]
        carry_scr[...] = carry
        for p in pages:
            acc = acc + wts[p][:, None, :] * v_refs[p][0]
        acc_scr[...] = acc

        @pl.when(j == ng - 1)
        def _():
            res = jnp.sum(acc_scr[...], axis=2, keepdims=True)
            o_ref[0] = res / l_scr[...][:, :, None] if fox else res

    def page_map(p, ndim):
        def index_map(i, j, pt):
            return (page_base + pt[i, (ng - 1 - j) * pps + (pps - 1 - p)],) + (0,) * (ndim - 1)
        return index_map

    col_spec = pl.BlockSpec((1, nh, hd, 1), lambda i, j, pt: (i, 0, 0, 0))
    in_specs = [col_spec]
    in_specs += [pl.BlockSpec((1, nh, hd, page), page_map(p, 4)) for p in range(pps)]
    in_specs += [pl.BlockSpec((1, nh, hd, page), page_map(p, 4)) for p in range(pps)]
    args = [q_col] + [cache_kt] * pps + [cache_vt] * pps
    scratch = [pltpu.VMEM((nh, hd, page), F32), pltpu.VMEM((nh, 1), F32)]
    if fox:
        in_specs += [pl.BlockSpec((1, nh, page), page_map(p, 3)) for p in range(pps)]
        in_specs += [col_spec, col_spec, pl.BlockSpec((1, nh, 1), lambda i, j, pt: (i, 0, 0))]
        args += [cache_lft] * pps + [k_new, v_new, logf_new]
        scratch += [pltpu.VMEM((nh, 1), F32), pltpu.VMEM((nh, 1), F32)]
    return pl.pallas_call(
        body,
        grid_spec=pltpu.PrefetchScalarGridSpec(
            num_scalar_prefetch=1, grid=(b, ng), in_specs=in_specs, out_specs=col_spec, scratch_shapes=scratch),
        out_shape=jax.ShapeDtypeStruct((b, nh, hd, 1), F32),
        compiler_params=_cparams("parallel", "arbitrary"), name="paged_sweep_" + mode)(page_table, *args)


def kernel(x_prompt, x_sample, state_ret, cache_sb_k, cache_sb_v, cache_fox_k, cache_fox_v, cache_fox_logf, state_gdn, state_conv, cache_mem_k, cache_mem_v, page_table, mem_prompt, g_mix, ev_w_in, ev_g_ret, ev_w_out, od_w_in, od_b_forget, od_conv_w, od_a_log, od_dt_bias, od_g_gdn, od_w_out, g_xattn, g_mem, w_mq, w_mk, w_mv, w_mo, g_ffn, w_route_group, w_route_expert, w_exp_gate, w_exp_up, w_exp_down, g_final):
    bp, t, d = x_prompt.shape
    bs, ts, _ = x_sample.shape
    assert bp == 1 and ts == 1 and t >= CONV_W - 1
    depth = g_mix.shape[0]
    n_pages = page_table.shape[1]
    n_phys, page = cache_sb_k.shape[1], cache_sb_k.shape[2]
    past_len = n_pages * page
    w = MIX_HALF
    nh, hd = N_HEADS, HEAD_DIM
    attn_scale = hd ** -0.5 * LOG2E
    f32o = [(F32, 1.0)]
    both = [(F32, 1.0), (BF16, 1.0)]
    row2 = lambda v: v.reshape(1, -1)

    def token_minor(cache):
        return jnp.transpose(cache, (0, 1, 3, 4, 2)).reshape(-1, nh, hd, page)

    half = hd // 2
    inv = ROPE_BASE ** (-jnp.arange(half, dtype=F32) / half)
    inv_lane = jnp.tile(inv, LANES // half)[None]
    inv_col = jnp.tile(inv, w // half)[:, None]

    yp = x_prompt[0]
    ys = x_sample[:, 0]
    y_prompt = y_sample = None
    ret_p, sbk_p, sbv_p, foxk_p, foxv_p, foxl_p, gdn_p, conv_p, memk_p, memv_p = ([] for _ in range(10))
    ret_s, sbk_s, sbv_s, foxk_s, foxv_s, foxl_s, gdn_s, conv_s = ([] for _ in range(8))

    for layer in range(depth):
        i = layer // 2
        g_in = row2(g_mix[layer])
        if layer % 2 == 0:
            w_in = ev_w_in[i]
            w_out = ev_w_out[i]
            outs = [(0, w, f32o), (w, w, f32o), (2 * w, w, f32o), (3 * w, w, f32o),
                    (4 * w, w, [(BF16, attn_scale)]), (5 * w, w, both), (6 * w, w, both)]
            rq, rk, rv, rg, sqb, sk, skb, sv, svb = _norm_proj(yp, g_in, w_in.astype(BF16), outs)
            o_ret, s_ret = _retention_prompt(rq, rk, rv, rg, row2(ev_g_ret[i]), inv_lane)
            o_sb = _sb_prompt(sqb, skb, svb)
            w_out_b = w_out.astype(BF16)
            yp = _resid_proj(yp, [o_ret, o_sb], [w_out_b[:w], w_out_b[w:]])
            ret_p.append(s_ret[None])
            sbk_p.append(sk.reshape(1, t, nh, hd))
            sbv_p.append(sv.reshape(1, t, nh, hd))
            outs = [(c * w, w, f32o) for c in range(7)]
            rq, rk, rv, rg, sq, sk, sv = _norm_proj(ys, g_in, w_in, outs, precise=True)
            o_ret, s_new = _retention_sample(
                rq.reshape(bs, w, 1), rk.reshape(bs, w, 1), rv.reshape(bs, nh, hd), rg.reshape(bs, nh, hd),
                state_ret[i].reshape(bs, w, hd), ev_g_ret[i].reshape(nh, hd), inv_col, past_len)
            o_sb = _paged_sweep("sb", sq.reshape(bs, nh, hd, 1), token_minor(cache_sb_k), token_minor(cache_sb_v),
                                page_table, i * n_phys)
            ys = _resid_proj(ys, [o_ret.reshape(bs, w), o_sb.reshape(bs, w)], [w_out[:w], w_out[w:]], precise=True)
            ret_s.append(s_new.reshape(bs, nh, hd, hd))
            sbk_s.append(sk.reshape(bs, 1, nh, hd))
            sbv_s.append(sv.reshape(bs, 1, nh, hd))
        else:
            w_full = od_w_in[i]
            w_out = od_w_out[i]
            c_fl = 3 * w
            c_qkv = c_fl + nh
            c_dz = c_qkv + 3 * w
            c_da = c_dz + w
            n_small = 3 * nh
            small_w = jnp.concatenate([w_full[:, c_fl:c_qkv], w_full[:, c_da:c_da + 2 * nh],
                                       jnp.zeros((d, LANES - n_small), F32)], axis=1)
            w_in = jnp.concatenate([w_full[:, :c_fl], w_full[:, c_qkv:c_da], small_w], axis=1)
            o_dqkv, o_dz, o_small = 3 * w, 6 * w, 7 * w
            zeros8 = jnp.zeros((nh,), F32)
            bias = jnp.concatenate([od_b_forget[i], od_dt_bias[i], zeros8])
            alog = jnp.concatenate([zeros8, od_a_log[i], zeros8])
            n_rows = 4 * nh
            bias_lane = jnp.pad(bias, (0, LANES - n_small))[None]
            alog_lane = jnp.pad(alog, (0, LANES - n_small))[None]
            bias_col = jnp.pad(bias, (0, n_rows - n_small))[:, None]
            alog_col = jnp.pad(alog, (0, n_rows - n_small))[:, None]
            outs = [(0, w, [(BF16, attn_scale)]), (w, w, both), (2 * w, w, both),
                    (o_dqkv, 3 * w, f32o), (o_dz, w, f32o), (o_small, LANES, f32o)]
            w_in_b = w_in.astype(BF16)
            small_wt = small_w[:, :n_rows].T.astype(BF16)
            fqb, fk, fkb, fv, fvb, dqkv, dz, small_col, small_row = _norm_proj(yp, g_in, w_in_b, outs, wt=small_wt)
            post_col, post_row = _gates(small_col, small_row, bias_lane, alog_lane, bias_col, alog_col)
            n_chunks = t // LANES
            cum_row = _cum_forget(post_row[:nh].reshape(nh * n_chunks, LANES), n_chunks).reshape(nh, t)
            o_fox = _fox_prompt(fqb, fkb, fvb, cum_row)
            gq, gk, gkt, gv = _gdn_prep(dqkv, od_conv_w[i])
            wy = _gdn_wy(gq, gk, gkt, gv, post_col, post_row)
            o_gdn, s_gdn = _gdn_scan(*wy, dz, row2(od_g_gdn[i]))
            w_out_b = w_out.astype(BF16)
            yp = _resid_proj(yp, [o_fox, o_gdn], [w_out_b[:w], w_out_b[w:]])
            foxk_p.append(fk.reshape(1, t, nh, hd))
            foxv_p.append(fv.reshape(1, t, nh, hd))
            foxl_p.append(post_col[:, :nh].reshape(1, t, nh))
            gdn_p.append(s_gdn[None])
            conv_p.append(dqkv[t - (CONV_W - 1):][None])
            outs = [(0, w, f32o), (w, w, f32o), (2 * w, w, f32o), (o_dqkv, 3 * w, f32o), (o_dz, w, f32o),
                    (o_small, LANES, f32o)]
            fq, fk, fv, dqkv, dz, small = _norm_proj(ys, g_in, w_in, outs, precise=True)
            post_s, _ = _gates(small, small[:, :n_rows].T, bias_lane, alog_lane, bias_col, alog_col)
            o_fox = _paged_sweep(
                "fox", fq.reshape(bs, nh, hd, 1), token_minor(cache_fox_k), token_minor(cache_fox_v),
                page_table, i * n_phys, k_new=fk.reshape(bs, nh, hd, 1), v_new=fv.reshape(bs, nh, hd, 1),
                cache_lft=jnp.swapaxes(cache_fox_logf, 2, 3).reshape(-1, nh, page),
                logf_new=post_s[:, :nh].reshape(bs, nh, 1))
            buf = state_conv[i]
            cw = od_conv_w[i]
            o_gdn, s_new = _gdn_sample(
                dqkv[:, :2 * w].reshape(bs, 2 * w, 1), buf[:, :, :2 * w].reshape(bs, CONV_W - 1, 2 * w, 1),
                cw[:, :2 * w].reshape(CONV_W, 2 * w, 1), dqkv[:, 2 * w:].reshape(bs, nh, hd),
                buf[:, :, 2 * w:].reshape(bs, CONV_W - 1, nh, hd), cw[:, 2 * w:].reshape(CONV_W, nh, hd),
                dz.reshape(bs, nh, hd), small[:, :n_rows].reshape(bs, n_rows, 1), bias_col, alog_col,
                state_gdn[i].reshape(bs, w, hd), row2(od_g_gdn[i]))
            ys = _resid_proj(ys, [o_fox.reshape(bs, w), o_gdn.reshape(bs, w)], [w_out[:w], w_out[w:]], precise=True)
            foxk_s.append(fk.reshape(bs, 1, nh, hd))
            foxv_s.append(fv.reshape(bs, 1, nh, hd))
            foxl_s.append(post_s[:, :nh].reshape(bs, 1, nh))
            gdn_s.append(s_new.reshape(bs, nh, hd, hd))
            conv_s.append(jnp.concatenate([buf[:, 1:], dqkv[:, None]], axis=1))

        n_mem = mem_prompt.shape[1]
        mw = MEM_HEADS * MEM_HEAD_DIM
        w_kv = jnp.concatenate([w_mk[layer], w_mv[layer]], axis=1).astype(BF16)
        mk, mkb, mv, mvb = _norm_proj(mem_prompt[0], row2(g_mem[layer]), w_kv, [(0, mw, both), (mw, mw, both)])
        memk_p.append(mk.reshape(1, n_mem, MEM_HEADS, MEM_HEAD_DIM))
        memv_p.append(mv.reshape(1, n_mem, MEM_HEADS, MEM_HEAD_DIM))
        g_x = row2(g_xattn[layer])
        yp = _mem_attn_prompt(yp, g_x, w_mq[layer].astype(BF16), mkb, mvb, w_mo[layer].astype(BF16))
        (q_s,) = _norm_proj(ys, g_x, w_mq[layer], [(0, mw, f32o)], precise=True)
        o_mem = _mem_attn_sample(q_s.reshape(bs, 1, mw), cache_mem_k[layer].reshape(bs, n_mem, mw),
                                 cache_mem_v[layer].reshape(bs, n_mem, mw))
        ys = _resid_proj(ys, [o_mem.reshape(bs, mw)], [w_mo[layer]], precise=True)

        w_route = jnp.concatenate([w_route_group[layer], w_route_expert[layer],
                                   jnp.zeros((d, LANES - N_GROUPS - N_EXPERTS), F32)], axis=1)
        g_f = row2(g_ffn[layer])
        fin = row2(g_final) if layer == depth - 1 else None
        res_p = _moe_dense(yp, g_f, w_route.astype(BF16), w_exp_gate[layer].astype(BF16),
                           w_exp_up[layer].astype(BF16), w_exp_down[layer].astype(BF16), final_g=fin)
        res_s = _moe_dense(ys, g_f, w_route, w_exp_gate[layer], w_exp_up[layer], w_exp_down[layer],
                           precise=True, final_g=fin)
        if fin is None:
            yp, ys = res_p, res_s
        else:
            (yp, y_prompt), (ys, y_sample) = res_p, res_s

    return (y_prompt[None], y_sample[:, None],
            jnp.stack(ret_p), jnp.stack(sbk_p), jnp.stack(sbv_p),
            jnp.stack(foxk_p), jnp.stack(foxv_p), jnp.stack(foxl_p), jnp.stack(gdn_p), jnp.stack(conv_p),
            jnp.stack(memk_p), jnp.stack(memv_p),
            jnp.stack(ret_s), jnp.stack(sbk_s), jnp.stack(sbv_s),
            jnp.stack(foxk_s), jnp.stack(foxv_s), jnp.stack(foxl_s), jnp.stack(gdn_s), jnp.stack(conv_s))
```

```python
import functools
import math

import jax
import jax.numpy as jnp
from jax import lax
from jax.experimental import pallas as pl
from jax.experimental.pallas import tpu as pltpu

F32 = jnp.float32
BF16 = jnp.bfloat16
I32 = jnp.int32

HEAD_DIM = 64
N_HEADS = 8
MIX_HALF = N_HEADS * HEAD_DIM
MEM_HEADS = 4
MEM_HEAD_DIM = 128
N_GROUPS = 4
EXPERTS_PER_GROUP = 4
N_EXPERTS = N_GROUPS * EXPERTS_PER_GROUP
CONV_W = 4
RET_CHUNK = 128
GDN_CHUNK = 64
GDN_INV_BASE = 8
ROPE_BASE = 10000.0
EPS = 1e-6
LOG2E = 1.4426950408889634
LN2 = 0.6931471805599453

LANES = 128
SUBLANES = 8
VMEM_LIMIT_BYTES = 56 * 1024 * 1024

HIGHEST = lax.Precision.HIGHEST


def _cparams(*sem):
    return pltpu.CompilerParams(dimension_semantics=sem, vmem_limit_bytes=VMEM_LIMIT_BYTES)


def _dot(a, b):
    return jnp.dot(a, b, preferred_element_type=F32)


def _dot_nt(a, b):
    return lax.dot_general(a, b, (((1,), (1,)), ((), ())), preferred_element_type=F32)


def _dot_hi(a, b):
    return jnp.dot(a, b, preferred_element_type=F32, precision=HIGHEST)


def _dot_nt_hi(a, b):
    return lax.dot_general(a, b, (((1,), (1,)), ((), ())), preferred_element_type=F32, precision=HIGHEST)


def _split2(x):
    hi = x.astype(BF16)
    lo = (x - hi.astype(F32)).astype(BF16)
    return hi, lo


def _split3(x):
    hi = x.astype(BF16)
    r = x - hi.astype(F32)
    mid = r.astype(BF16)
    lo = (r - mid.astype(F32)).astype(BF16)
    return hi, mid, lo


def _dot_x3(x, m_bf16):
    hi, mid, lo = _split3(x)
    return _dot(hi, m_bf16) + _dot(mid, m_bf16) + _dot(lo, m_bf16)


def _dot_3x(m_bf16, x):
    hi, mid, lo = _split3(x)
    return _dot(m_bf16, hi) + _dot(m_bf16, mid) + _dot(m_bf16, lo)


def _rms(x, g):
    return x * lax.rsqrt(jnp.mean(x * x, axis=-1, keepdims=True) + EPS) * g


def _silu(x):
    return x / (1.0 + jnp.exp(-x))


def _softplus(x):
    return jnp.maximum(x, 0.0) + jnp.log(1.0 + jnp.exp(-jnp.abs(x)))


def _iota(shape, dim):
    return lax.broadcasted_iota(I32, shape, dim)


def _head_block_ones(n, dtype=BF16):
    r = _iota((n, n), 0) // HEAD_DIM
    c = _iota((n, n), 1) // HEAD_DIM
    return jnp.where(r == c, 1.0, 0.0).astype(dtype)


def _norm_proj(x, g, w, outs, *, wt=None, precise=False, tm=512):
    m, d = x.shape
    tm = min(tm, m)
    assert m % tm == 0
    n_out = sum(len(o[2]) for o in outs)

    def body(*refs):
        x_ref, g_ref, w_ref = refs[:3]
        pos = 3
        wt_ref = None
        if wt is not None:
            wt_ref = refs[3]
            pos = 4
        o_refs = refs[pos:]
        xn = _rms(x_ref[...], g_ref[...])
        xb = xn if precise else xn.astype(BF16)
        k = 0
        for start, width, variants in outs:
            wslab = w_ref[:, start:start + width]
            r = _dot_hi(xb, wslab) if precise else _dot(xb, wslab)
            for dtype, scale in variants:
                o_refs[k][...] = (r if scale == 1.0 else r * scale).astype(dtype)
                k += 1
        if wt_ref is not None:
            o_refs[k][...] = _dot_nt_hi(wt_ref[...], xb) if precise else _dot_nt(wt_ref[...], xb)

    in_specs = [pl.BlockSpec((tm, d), lambda i: (i, 0)),
                pl.BlockSpec((1, d), lambda i: (0, 0)),
                pl.BlockSpec(w.shape, lambda i: (0, 0))]
    args = [x, g, w]
    if wt is not None:
        in_specs.append(pl.BlockSpec(wt.shape, lambda i: (0, 0)))
        args.append(wt)
    out_shape, out_specs = [], []
    for start, width, variants in outs:
        for dtype, _ in variants:
            out_shape.append(jax.ShapeDtypeStruct((m, width), dtype))
            out_specs.append(pl.BlockSpec((tm, width), lambda i: (i, 0)))
    if wt is not None:
        out_shape.append(jax.ShapeDtypeStruct((wt.shape[0], m), F32))
        out_specs.append(pl.BlockSpec((wt.shape[0], tm), lambda i: (0, i)))
    res = pl.pallas_call(
        body, grid=(m // tm,), in_specs=in_specs, out_specs=out_specs, out_shape=out_shape,
        compiler_params=_cparams("parallel"), name="norm_proj")(*args)
    assert len(res) == n_out + (wt is not None)
    return res


def _resid_proj(res, a_list, w_list, *, precise=False, tm=512, final_g=None):
    m, d = res.shape
    tm = min(tm, m)
    n = len(a_list)

    def body(*refs):
        r_ref = refs[0]
        a_refs = refs[1:1 + n]
        w_refs = refs[1 + n:1 + 2 * n]
        pos = 1 + 2 * n
        acc = r_ref[...]
        for a_ref, w_ref in zip(a_refs, w_refs):
            acc = acc + (_dot_hi(a_ref[...], w_ref[...]) if precise else _dot(a_ref[...], w_ref[...]))
        if final_g is None:
            refs[pos][...] = acc
        else:
            refs[pos + 1][...] = acc
            refs[pos + 2][...] = _rms(acc, refs[pos][...])

    in_specs = [pl.BlockSpec((tm, d), lambda i: (i, 0))]
    in_specs += [pl.BlockSpec((tm, a.shape[1]), lambda i: (i, 0)) for a in a_list]
    in_specs += [pl.BlockSpec(w.shape, lambda i: (0, 0)) for w in w_list]
    args = [res, *a_list, *w_list]
    out_shape = [jax.ShapeDtypeStruct((m, d), F32)]
    out_specs = [pl.BlockSpec((tm, d), lambda i: (i, 0))]
    if final_g is not None:
        in_specs.append(pl.BlockSpec((1, d), lambda i: (0, 0)))
        args.append(final_g)
        out_shape.append(jax.ShapeDtypeStruct((m, d), F32))
        out_specs.append(pl.BlockSpec((tm, d), lambda i: (i, 0)))
    out = pl.pallas_call(
        body, grid=(m // tm,), in_specs=in_specs, out_specs=out_specs, out_shape=out_shape,
        compiler_params=_cparams("parallel"), name="resid_proj")(*args)
    return out[0] if final_g is None else out


def _rope_tables(pos, inv_lane):
    ang = pos * inv_lane
    return jnp.cos(ang), jnp.sin(ang)


def _rotary_lanes(x, cos, sin, first_half):
    w = x.shape[-1]
    reps = w // LANES
    c = jnp.concatenate([cos] * reps, axis=-1) if reps > 1 else cos
    s = jnp.concatenate([sin] * reps, axis=-1) if reps > 1 else sin
    x_up = pltpu.roll(x, w - HEAD_DIM // 2, 1)
    x_dn = pltpu.roll(x, HEAD_DIM // 2, 1)
    return x * c + jnp.where(first_half, -x_up, x_dn) * s


def _retention_prompt(rq, rk, rv, rg, g_ret, inv_lane):
    t, w = rq.shape
    c = math.gcd(t, RET_CHUNK)
    n = t // c
    scale = HEAD_DIM ** -0.5

    def body(q_ref, k_ref, v_ref, gate_ref, gret_ref, inv_ref, o_ref, s_out_ref,
             s_scr, dec_scr, qd_scr, kd_scr):
        i = pl.program_id(0)
        lane = _iota((c, w), 1)
        head_of_lane = lane // HEAD_DIM
        first_half = (lane % HEAD_DIM) < (HEAD_DIM // 2)
        lg_lane = jnp.log(1.0 - jnp.exp2(-5.0 - head_of_lane.astype(F32)))
        row = _iota((c, w), 0).astype(F32)

        @pl.when(i == 0)
        def _():
            s_scr[...] = jnp.zeros_like(s_scr)
            qd_scr[...] = jnp.exp((row + 1.0) * lg_lane)
            kd_scr[...] = jnp.exp((c - 1.0 - row) * lg_lane)
            ri = _iota((c, c), 0)
            ci = _iota((c, c), 1)
            diff = (ri - ci).astype(F32)
            for h in range(N_HEADS):
                lg_h = math.log(1.0 - 2.0 ** (-5.0 - h))
                dec_scr[h] = jnp.where(ri >= ci, jnp.exp(jnp.where(ri >= ci, diff, 0.0) * lg_h), 0.0)

        pos = (i * c).astype(F32) + _iota((c, 1), 0).astype(F32)
        cos, sin = _rope_tables(pos, inv_ref[...])
        q = _rotary_lanes(q_ref[...], cos, sin, first_half)
        k = _rotary_lanes(k_ref[...], cos, sin, first_half) * scale
        qb = q.astype(BF16)
        kb = k.astype(BF16)
        vb = v_ref[...].astype(BF16)
        s_old = s_scr[...]
        o = _dot(qb, s_old.astype(BF16)) * qd_scr[...]
        hms = [head_of_lane == h for h in range(N_HEADS)]
        scs = [_dot_nt(jnp.where(hm, qb, jnp.zeros_like(qb)), kb) for hm in hms]
        ohs = [_dot((scs[h] * dec_scr[h]).astype(BF16), vb) for h in range(N_HEADS)]
        for h in range(N_HEADS):
            o = o + jnp.where(hms[h], ohs[h], 0.0)
        kd = (k * kd_scr[...]).astype(BF16)
        upd = lax.dot_general(kd, vb, (((0,), (0,)), ((), ())), preferred_element_type=F32)
        blk = (_iota((w, w), 0) // HEAD_DIM) == (_iota((w, w), 1) // HEAD_DIM)
        cd = jnp.exp(float(c) * jnp.log(1.0 - jnp.exp2(-5.0 - (_iota((1, w), 1) // HEAD_DIM).astype(F32))))
        s_scr[...] = s_old * cd + jnp.where(blk, upd, 0.0)
        ones = _head_block_ones(w)
        inv_n = 1.0 / HEAD_DIM
        hi, lo = _split2(o)
        mu = (_dot(hi, ones) + _dot(lo, ones)) * inv_n
        dlt = o - mu
        hi, lo = _split2(dlt * dlt)
        var = (_dot(hi, ones) + _dot(lo, ones)) * inv_n
        y = dlt * lax.rsqrt(var + EPS) * gret_ref[...]
        o_ref[...] = (y * _silu(gate_ref[...])).astype(o_ref.dtype)

        @pl.when(i == n - 1)
        def _():
            s_fin = s_scr[...]
            for h in range(N_HEADS):
                s_out_ref[h] = s_fin[h * HEAD_DIM:(h + 1) * HEAD_DIM, h * HEAD_DIM:(h + 1) * HEAD_DIM]

    tok = pl.BlockSpec((c, w), lambda i: (i, 0))
    o, s = pl.pallas_call(
        body, grid=(n,),
        in_specs=[tok, tok, tok, tok, pl.BlockSpec((1, w), lambda i: (0, 0)),
                  pl.BlockSpec((1, LANES), lambda i: (0, 0))],
        out_specs=[tok, pl.BlockSpec((N_HEADS, HEAD_DIM, HEAD_DIM), lambda i: (0, 0, 0))],
        out_shape=[jax.ShapeDtypeStruct((t, w), BF16),
                   jax.ShapeDtypeStruct((N_HEADS, HEAD_DIM, HEAD_DIM), F32)],
        scratch_shapes=[pltpu.VMEM((w, w), F32), pltpu.VMEM((N_HEADS, c, c), F32),
                        pltpu.VMEM((c, w), F32), pltpu.VMEM((c, w), F32)],
        compiler_params=_cparams("arbitrary"), name="retention_prompt")(rq, rk, rv, rg, g_ret, inv_lane)
    return o, s


ATTN_BLOCKS = (512, 256, 128)
SB_CUM_BLOCK = 256
FOX_COL_BLOCK = 256


def _attn_block(t):
    for b in ATTN_BLOCKS:
        if t % b == 0:
            return b
    raise ValueError(f"sequence length {t} must be a multiple of {ATTN_BLOCKS[-1]}")


SOFTPLUS2_LINEAR_ABOVE = 60.0


def _softplus2(z):
    return jnp.where(z > SOFTPLUS2_LINEAR_ABOVE, z, jnp.log(1.0 + jnp.exp2(z)) * LOG2E)


def _sb_prompt(qb, kb, vb):
    t, w = qb.shape
    blk = _attn_block(t)
    cb = min(blk, SB_CUM_BLOCK)
    nblk = blk // cb
    nq = t // blk
    n_pairs = w // LANES

    def body(q_ref, k_ref, v_ref, o_ref, acc_scr, c_scr):
        qi = pl.program_id(1)
        q = q_ref[...]
        lane = _iota((blk, LANES), 1)
        zero = jnp.zeros_like(q)
        q_heads = (jnp.where(lane < HEAD_DIM, q, zero), jnp.where(lane >= HEAD_DIM, q, zero))
        incl = jnp.where(_iota((2 * cb, cb), 0) % cb >= _iota((2 * cb, cb), 1), 1.0, 0.0).astype(BF16)
        acc_scr[...] = jnp.zeros_like(acc_scr)
        c_scr[...] = jnp.zeros_like(c_scr)

        heads = range(2)

        def blocks(start, strict, nb=nblk):
            order = [(b, hh) for b in reversed(range(nb)) for hh in heads]
            ks = {b: k_ref[pl.ds(start + b * cb, cb), :] for b in range(nb)}
            vs = {b: v_ref[pl.ds(start + b * cb, cb), :] for b in range(nb)}
            masks = {b: None if strict is None else strict[:, b * cb:(b + 1) * cb] for b in range(nb)}
            zs = {c: _dot_nt(q_heads[c[1]], ks[c[0]]) for c in order}
            sps = {c: _softplus2(zs[c]) for c in order}
            if strict is not None:
                sps = {c: jnp.where(masks[c[0]], sps[c], 0.0) for c in order}
            laters = {c: _dot(sps[c].astype(BF16), incl[:cb]) for c in order}
            carries = [c_scr[hh] for hh in heads]
            wgts = {}
            for c in order:
                b, hh = c
                wgt = jnp.exp2(zs[c] - laters[c] - carries[hh])
                wgts[c] = (wgt if strict is None else jnp.where(masks[b], wgt, 0.0)).astype(BF16)
                carries[hh] = carries[hh] + jnp.sum(sps[c], axis=1, keepdims=True)
            outs = {c: _dot(wgts[c], vs[c[0]]) for c in order}
            for hh in heads:
                tot = outs[(0, hh)]
                for b in range(1, nb):
                    tot = tot + outs[(b, hh)]
                acc_scr[hh] += tot
                c_scr[hh] = carries[hh]

        blocks(pl.multiple_of(qi * blk, blk), _iota((blk, blk), 1) < _iota((blk, blk), 0))

        def step(j, _):
            blocks(pl.multiple_of((qi - 2 - 2 * j) * blk, blk), None, 2 * nblk)
            return 0

        lax.fori_loop(0, qi // 2, step, 0)

        @pl.when(qi % 2 == 1)
        def _():
            blocks(0, None)

        o_ref[...] = jnp.where(lane < HEAD_DIM, acc_scr[0], acc_scr[1]).astype(o_ref.dtype)

    return pl.pallas_call(
        body, grid=(n_pairs, nq),
        in_specs=[pl.BlockSpec((blk, LANES), lambda p, i: (i, p)),
                  pl.BlockSpec((t, LANES), lambda p, i: (0, p)),
                  pl.BlockSpec((t, LANES), lambda p, i: (0, p))],
        out_specs=pl.BlockSpec((blk, LANES), lambda p, i: (i, p)),
        out_shape=jax.ShapeDtypeStruct((t, w), BF16),
        scratch_shapes=[pltpu.VMEM((2, blk, LANES), F32), pltpu.VMEM((2, blk, 1), F32)],
        compiler_params=_cparams("parallel", "arbitrary"), name="sb_prompt")(qb, kb, vb)


def _fox_prompt(qb, kb, vb, cum_row):
    t, w = qb.shape
    blk = _attn_block(t)
    cb = min(blk, FOX_COL_BLOCK)
    n_cb = blk // cb
    nq = t // blk
    n_pairs = w // LANES

    def body(q_ref, k_ref, v_ref, ck_ref, o_ref, acc_scr, m_scr, z_scr):
        p = pl.program_id(0)
        qi = pl.program_id(1)
        q = q_ref[...]
        lane = _iota((blk, LANES), 1)
        first = lane < HEAD_DIM
        zero = jnp.zeros_like(q)
        q_heads = (jnp.where(first, q, zero), jnp.where(first, zero, q))
        acc_scr[...] = jnp.zeros_like(acc_scr)
        m_scr[...] = jnp.full_like(m_scr, -jnp.inf)
        rsel = _iota((N_HEADS, blk), 0)
        heads = range(2)

        def logits(start):
            k = k_ref[pl.ds(start, blk), :]
            return [_dot_nt(q_heads[hh], k) for hh in heads]

        def blocks(start, causal, zs, next_start, next_slot):
            z_next = logits(next_start)
            for hh in heads:
                z_scr[next_slot, hh] = z_next[hh]
            v = v_ref[pl.ds(start, blk), :]
            ck_all = ck_ref[:, pl.ds(start, blk)]
            one = jnp.ones_like(v)
            v_aug = (jnp.where(first, v, one), jnp.where(first, one, v))
            ss = []
            for hh in heads:
                ck = jnp.sum(jnp.where(rsel == 2 * p + hh, ck_all, 0.0), axis=0, keepdims=True)
                s = zs[hh] - ck
                ss.append(s if causal is None else jnp.where(causal, s, -jnp.inf))
            maxes = [jnp.max(s, axis=1, keepdims=True) for s in ss]
            m_olds = [m_scr[hh] for hh in heads]
            m_news = [jnp.maximum(m_olds[hh], maxes[hh]) for hh in heads]
            alphas = [jnp.exp2(m_olds[hh] - m_news[hh]) for hh in heads]
            order = [(b, hh) for b in range(n_cb) for hh in heads]
            prs = {(b, hh): jnp.exp2(ss[hh][:, b * cb:(b + 1) * cb] - m_news[hh]).astype(BF16) for b, hh in order}
            outs = {(b, hh): _dot(prs[(b, hh)], v_aug[hh][b * cb:(b + 1) * cb]) for b, hh in order}
            for hh in heads:
                tot = outs[(0, hh)]
                for b in range(1, n_cb):
                    tot = tot + outs[(b, hh)]
                acc_scr[hh] = alphas[hh] * acc_scr[hh] + tot
                m_scr[hh] = m_news[hh]

        def start_of(block):
            return pl.multiple_of(jnp.maximum(block, 0) * blk, blk)

        blocks(start_of(qi), _iota((blk, blk), 1) <= _iota((blk, blk), 0), logits(start_of(qi)), start_of(qi - 1), 0)

        def step(j, _):
            slot = j % 2
            blocks(start_of(qi - 1 - j), None, [z_scr[slot, hh] for hh in heads], start_of(qi - 2 - j), 1 - slot)
            return 0

        lax.fori_loop(0, qi, step, 0)
        outs = [acc_scr[hh] / pltpu.roll(acc_scr[hh], HEAD_DIM, 1) for hh in heads]
        o_ref[...] = jnp.where(first, outs[0], outs[1]).astype(o_ref.dtype)

    return pl.pallas_call(
        body, grid=(n_pairs, nq),
        in_specs=[pl.BlockSpec((blk, LANES), lambda p, i: (i, p)),
                  pl.BlockSpec((t, LANES), lambda p, i: (0, p)),
                  pl.BlockSpec((t, LANES), lambda p, i: (0, p)),
                  pl.BlockSpec((N_HEADS, t), lambda p, i: (0, 0))],
        out_specs=pl.BlockSpec((blk, LANES), lambda p, i: (i, p)),
        out_shape=jax.ShapeDtypeStruct((t, w), BF16),
        scratch_shapes=[pltpu.VMEM((2, blk, LANES), F32), pltpu.VMEM((2, blk, 1), F32),
                        pltpu.VMEM((2, 2, blk, blk), F32)],
        compiler_params=_cparams("parallel", "arbitrary"), name="fox_prompt")(qb, kb, vb, cum_row)


def _gate_math(x, bias, a_log, idx):
    xb = x + bias
    e = jnp.exp(-jnp.abs(xb))
    l1p = jnp.log(1.0 + e)
    logf = jnp.minimum(xb, 0.0) - l1p
    g = -jnp.exp(a_log) * (jnp.maximum(xb, 0.0) + l1p)
    beta = 1.0 / (1.0 + jnp.exp(-xb))
    h = N_HEADS
    return jnp.where(idx < h, logf, jnp.where(idx < 2 * h, g, jnp.where(idx < 3 * h, beta, 0.0)))


def _gates(small_col, small_row, bias_lane, alog_lane, bias_col, alog_col, *, tm=2048):
    m = small_col.shape[0]
    r = small_row.shape[0]
    tm = min(tm, m)

    def body(c_ref, r_ref, bl_ref, al_ref, bc_ref, ac_ref, oc_ref, or_ref):
        oc_ref[...] = _gate_math(c_ref[...], bl_ref[...], al_ref[...], _iota((tm, LANES), 1))
        or_ref[...] = _gate_math(r_ref[...], bc_ref[...], ac_ref[...], _iota((r, tm), 0))

    return pl.pallas_call(
        body, grid=(m // tm,),
        in_specs=[pl.BlockSpec((tm, LANES), lambda i: (i, 0)), pl.BlockSpec((r, tm), lambda i: (0, i)),
                  pl.BlockSpec((1, LANES), lambda i: (0, 0)), pl.BlockSpec((1, LANES), lambda i: (0, 0)),
                  pl.BlockSpec((r, 1), lambda i: (0, 0)), pl.BlockSpec((r, 1), lambda i: (0, 0))],
        out_specs=[pl.BlockSpec((tm, LANES), lambda i: (i, 0)), pl.BlockSpec((r, tm), lambda i: (0, i))],
        out_shape=[jax.ShapeDtypeStruct((m, LANES), F32), jax.ShapeDtypeStruct((r, m), F32)],
        compiler_params=_cparams("parallel"), name="gates")(
            small_col, small_row, bias_lane, alog_lane, bias_col, alog_col)


def _cum_forget(logf_rows, n_chunks):
    r = logf_rows.shape[0]

    def body(x_ref, o_ref):
        x = x_ref[...]
        incl = jnp.where(_iota((LANES, LANES), 0) <= _iota((LANES, LANES), 1), 1.0, 0.0).astype(BF16)
        local = _dot_x3(x, incl)
        tot = jnp.broadcast_to(local[:, LANES - 1:LANES], (r, LANES))
        ri = _iota((r, r), 0)
        ci = _iota((r, r), 1)
        before = jnp.where((ri // n_chunks == ci // n_chunks) & (ci < ri), 1.0, 0.0).astype(BF16)
        o_ref[...] = (local + _dot_3x(before, tot)) * LOG2E

    return pl.pallas_call(
        body, grid=(1,),
        in_specs=[pl.BlockSpec((r, LANES), lambda i: (0, 0))],
        out_specs=pl.BlockSpec((r, LANES), lambda i: (0, 0)),
        out_shape=jax.ShapeDtypeStruct((r, LANES), F32),
        compiler_params=_cparams("arbitrary"), name="cum_forget")(logf_rows)


def _gdn_prep(dqkv, conv_w, *, tm=256):
    t, w3 = dqkv.shape
    w = w3 // 3
    tm = min(tm, t)
    scale = HEAD_DIM ** -0.5

    def body(x_ref, prev_ref, cw_ref, q_ref, k_ref, kt_ref, v_ref):
        i = pl.program_id(0)
        x = x_ref[...]
        prev = jnp.where(i == 0, 0.0, prev_ref[...])
        xp = jnp.concatenate([prev, x], axis=0)
        cw = cw_ref[...]
        off = SUBLANES - (CONV_W - 1)
        y = xp[off:off + tm] * cw[0:1]
        for j in range(1, CONV_W):
            y = y + xp[off + j:off + j + tm] * cw[j:j + 1]
        y = _silu(y)
        ones = _head_block_ones(w)

        def l2n(a):
            hi, lo = _split2(a * a)
            return a * lax.rsqrt(_dot(hi, ones) + _dot(lo, ones) + EPS)

        qn = l2n(y[:, :w]) * scale
        kn = l2n(y[:, w:2 * w])
        vv = y[:, 2 * w:]
        for h in range(N_HEADS):
            sl = slice(h * HEAD_DIM, (h + 1) * HEAD_DIM)
            q_ref[h] = qn[:, sl]
            k_ref[h] = kn[:, sl]
            v_ref[h] = vv[:, sl]
        kt_ref[...] = kn.T

    hm = pl.BlockSpec((N_HEADS, tm, HEAD_DIM), lambda i: (0, i, 0))
    hm_shape = jax.ShapeDtypeStruct((N_HEADS, t, HEAD_DIM), F32)
    return pl.pallas_call(
        body, grid=(t // tm,),
        in_specs=[pl.BlockSpec((tm, w3), lambda i: (i, 0)),
                  pl.BlockSpec((SUBLANES, w3), lambda i: (jnp.maximum(i * (tm // SUBLANES) - 1, 0), 0)),
                  pl.BlockSpec((CONV_W, w3), lambda i: (0, 0))],
        out_specs=[hm, hm, pl.BlockSpec((w, tm), lambda i: (0, i)), hm],
        out_shape=[hm_shape, hm_shape, jax.ShapeDtypeStruct((w, t), F32), hm_shape],
        compiler_params=_cparams("parallel"), name="gdn_prep")(dqkv, dqkv, conv_w)


def _mm3(a, b):
    ah, al = _split2(a)
    bh, bl = _split2(b)
    return _dot(ah, bh) + _dot(ah, bl) + _dot(al, bh)


def _gdn_wy(q, k, kt, v, post_col, post_row):
    nh, t, d = q.shape
    c = math.gcd(t, GDN_CHUNK)
    span = LANES if t % LANES == 0 else t
    per = span // c

    def body(q_ref, k_ref, kt_ref, v_ref, pc_ref, pr_ref, u_ref, w_ref, qe_ref, qk_ref, kdt_ref, egl_ref):
        ri = _iota((c, c), 0)
        ci = _iota((c, c), 1)
        tril = ri >= ci
        strict = ri > ci
        lower_incl = jnp.where(tril, 1.0, 0.0).astype(BF16)
        upper_incl = jnp.where(ri <= ci, 1.0, 0.0).astype(BF16)
        xs, ys = [], []
        units = [(s, h) for s in range(per) for h in range(nh)]
        cums = []
        for s in range(per):
            rows = slice(s * c, (s + 1) * c)
            cums.append((_dot_3x(lower_incl, pc_ref[rows, :]),
                         _dot_x3(pr_ref[:, rows], upper_incl)))
        for s, h in units:
            rows = slice(s * c, (s + 1) * c)
            cg_col_all, cg_row_all = cums[s]
            cgc = cg_col_all[:, nh + h:nh + h + 1]
            cgr = cg_row_all[nh + h:nh + h + 1, :]
            beta = pc_ref[rows, 2 * nh + h:2 * nh + h + 1]
            g_last = cgr[:, c - 1:c]
            decay = jnp.where(tril, jnp.exp(jnp.where(tril, cgc - cgr, 0.0)), 0.0)
            qh = q_ref[h, rows, :]
            kth = kt_ref[h * d:(h + 1) * d, rows]
            k_beta = k_ref[h, rows, :] * beta
            gram = _dot(jnp.concatenate([k_beta, qh], axis=0).astype(BF16), kth.astype(BF16))
            qk_ref[h, rows, :] = jnp.where(tril, gram[c:] * decay, 0.0).astype(BF16)
            qe_ref[h, rows, :] = (qh * jnp.exp(cgc)).astype(BF16)
            kdt_ref[h * d:(h + 1) * d, rows] = (kth * jnp.exp(g_last - cgr)).astype(BF16)
            egl_ref[h, rows, :] = jnp.broadcast_to(jnp.exp(g_last), (c, d))
            xs.append(jnp.where(strict, gram[:c] * decay, 0.0))
            ys.append(jnp.concatenate([v_ref[h, rows, :] * beta, k_beta * jnp.exp(cgc)], axis=1))
        base = GDN_INV_BASE
        assert c % base == 0 and (c // base) & (c // base - 1) == 0
        eye = jnp.where(ri == ci, 1.0, 0.0)
        n_u = len(units)

        def same_block(size):
            return (ri // size) == (ci // size)

        ts = [eye - jnp.where(same_block(base), a, 0.0) for a in xs]
        pws = [t - eye for t in ts]
        span_done = 2
        while span_done < base:
            sq = [_mm3(pws[i], pws[i]) for i in range(n_u)]
            ts = [ts[i] + _mm3(ts[i], sq[i]) for i in range(n_u)]
            pws = sq
            span_done *= 2
        size = base
        while 2 * size < c:
            coup = same_block(2 * size) & ~same_block(size)
            prods = [_mm3(jnp.where(coup, xs[i], 0.0), ts[i]) for i in range(n_u)]
            ts = [ts[i] - _mm3(ts[i], prods[i]) for i in range(n_u)]
            size *= 2
        if size < c:
            coup = ~same_block(size)
            tys = [_mm3(ts[i], ys[i]) for i in range(n_u)]
            prods = [_mm3(jnp.where(coup, xs[i], 0.0), tys[i]) for i in range(n_u)]
            ys = [tys[i] - _mm3(ts[i], prods[i]) for i in range(n_u)]
        else:
            ys = [_mm3(ts[i], ys[i]) for i in range(n_u)]
        for i, (s, h) in enumerate(units):
            rows = slice(s * c, (s + 1) * c)
            u_ref[h, rows, :] = ys[i][:, :d]
            w_ref[h, rows, :] = ys[i][:, d:].astype(BF16)

    hm = pl.BlockSpec((nh, span, d), lambda i: (0, i, 0))
    ktb = pl.BlockSpec((nh * d, span), lambda i: (0, i))
    return pl.pallas_call(
        body, grid=(t // span,),
        in_specs=[hm, hm, ktb, hm, pl.BlockSpec((span, LANES), lambda i: (i, 0)),
                  pl.BlockSpec((post_row.shape[0], span), lambda i: (0, i))],
        out_specs=[hm, hm, hm, hm, ktb, hm],
        out_shape=[jax.ShapeDtypeStruct((nh, t, d), F32), jax.ShapeDtypeStruct((nh, t, d), BF16),
                   jax.ShapeDtypeStruct((nh, t, d), BF16), jax.ShapeDtypeStruct((nh, t, d), BF16),
                   jax.ShapeDtypeStruct((nh * d, t), BF16), jax.ShapeDtypeStruct((nh, t, d), F32)],
        compiler_params=_cparams("parallel"), name="gdn_wy")(q, k, kt, v, post_col, post_row)


def _gdn_scan(u, w, qe, qk, kdt, egl, dz, g_gdn):
    nh, t, d = u.shape
    c = math.gcd(t, GDN_CHUNK)
    span = LANES if t % LANES == 0 else t
    per = span // c
    n = t // span

    def body(u_ref, w_ref, qe_ref, qk_ref, kdt_ref, egl_ref, dz_ref, gg_ref, o_ref, s_out_ref, s_scr, o_scr):
        i = pl.program_id(0)

        @pl.when(i == 0)
        def _():
            s_scr[...] = jnp.zeros_like(s_scr)

        heads = range(nh)
        states = [s_scr[h] for h in heads]
        for j in range(per):
            rows = slice(j * c, (j + 1) * c)
            sbs = [states[h].astype(BF16) for h in heads]
            ws = [_dot(w_ref[h, rows, :], sbs[h]) for h in heads]
            qs = [_dot(qe_ref[h, rows, :], sbs[h]) for h in heads]
            vbs = [(u_ref[h, rows, :] - ws[h]).astype(BF16) for h in heads]
            os_ = [qs[h] + _dot(qk_ref[h, rows, :], vbs[h]) for h in heads]
            upd = [_dot(kdt_ref[h * d:(h + 1) * d, rows], vbs[h]) for h in heads]
            for h in heads:
                states[h] = states[h] * egl_ref[h, rows, :] + upd[h]
                o = os_[h]
                y = o * lax.rsqrt(jnp.mean(o * o, axis=-1, keepdims=True) + EPS) * gg_ref[...]
                o_scr[rows, h * d:(h + 1) * d] = y
        for h in heads:
            s_scr[h] = states[h]
        o_ref[...] = (o_scr[...] * _silu(dz_ref[...])).astype(o_ref.dtype)

        @pl.when(i == n - 1)
        def _():
            s_out_ref[...] = s_scr[...]

    hm = pl.BlockSpec((nh, span, d), lambda i: (0, i, 0))
    return pl.pallas_call(
        body, grid=(n,),
        in_specs=[hm, hm, hm, hm, pl.BlockSpec((nh * d, span), lambda i: (0, i)), hm,
                  pl.BlockSpec((span, nh * d), lambda i: (i, 0)), pl.BlockSpec((1, d), lambda i: (0, 0))],
        out_specs=[pl.BlockSpec((span, nh * d), lambda i: (i, 0)),
                   pl.BlockSpec((nh, d, d), lambda i: (0, 0, 0))],
        out_shape=[jax.ShapeDtypeStruct((t, nh * d), BF16), jax.ShapeDtypeStruct((nh, d, d), F32)],
        scratch_shapes=[pltpu.VMEM((nh, d, d), F32), pltpu.VMEM((span, nh * d), F32)],
        compiler_params=_cparams("arbitrary"), name="gdn_scan")(u, w, qe, qk, kdt, egl, dz, g_gdn)


def _mem_attn_core(q, mk, mv, precise):
    scale = MEM_HEAD_DIM ** -0.5
    outs = []
    for h in range(MEM_HEADS):
        sl = slice(h * MEM_HEAD_DIM, (h + 1) * MEM_HEAD_DIM)
        if precise:
            s = _dot_nt_hi(q[:, sl], mk[:, sl]) * scale
        else:
            s = _dot_nt(q[:, sl].astype(BF16), mk[:, sl]) * scale
        e = jnp.exp(s - jnp.max(s, axis=-1, keepdims=True))
        p = e / jnp.sum(e, axis=-1, keepdims=True)
        outs.append(_dot_hi(p, mv[:, sl]) if precise else _dot(p.astype(BF16), mv[:, sl]))
    return jnp.concatenate(outs, axis=-1)


def _mem_attn_prompt(y, g, wq, mk, mv, wo, *, tm=512):
    m, d = y.shape
    tm = min(tm, m)

    def body(y_ref, g_ref, wq_ref, mk_ref, mv_ref, wo_ref, o_ref):
        yv = y_ref[...]
        hb = _rms(yv, g_ref[...]).astype(BF16)
        q = _dot(hb, wq_ref[...])
        o = _mem_attn_core(q, mk_ref[...], mv_ref[...], False)
        o_ref[...] = yv + _dot(o.astype(BF16), wo_ref[...])

    full = lambda a: pl.BlockSpec(a.shape, lambda i: (0, 0))
    return pl.pallas_call(
        body, grid=(m // tm,),
        in_specs=[pl.BlockSpec((tm, d), lambda i: (i, 0)), pl.BlockSpec((1, d), lambda i: (0, 0)),
                  full(wq), full(mk), full(mv), full(wo)],
        out_specs=pl.BlockSpec((tm, d), lambda i: (i, 0)),
        out_shape=jax.ShapeDtypeStruct((m, d), F32),
        compiler_params=_cparams("parallel"), name="mem_attn_prompt")(y, g, wq, mk, mv, wo)


def _mem_attn_sample(q, mem_k, mem_v):
    b, _, w = q.shape
    n_mem = mem_k.shape[1]

    def body(q_ref, k_ref, v_ref, o_ref):
        q8 = jnp.broadcast_to(q_ref[0], (SUBLANES, w))
        o = _mem_attn_core(q8, k_ref[0], v_ref[0], True)
        o_ref[0] = o[0:1]

    return pl.pallas_call(
        body, grid=(b,),
        in_specs=[pl.BlockSpec((1, 1, w), lambda i: (i, 0, 0)),
                  pl.BlockSpec((1, n_mem, w), lambda i: (i, 0, 0)),
                  pl.BlockSpec((1, n_mem, w), lambda i: (i, 0, 0))],
        out_specs=pl.BlockSpec((1, 1, w), lambda i: (i, 0, 0)),
        out_shape=jax.ShapeDtypeStruct((b, 1, w), F32),
        compiler_params=_cparams("parallel"), name="mem_attn_sample")(q, mem_k, mem_v)


def _route(logits):
    m = logits.shape[0]
    lane = _iota((m, LANES), 1)
    big = LANES
    neg = -jnp.inf
    is_g = lane < N_GROUPS
    gl = jnp.where(is_g, logits, neg)
    gmax = jnp.max(gl, axis=1, keepdims=True)
    g_val = 1.0 / jnp.sum(jnp.where(is_g, jnp.exp(gl - gmax), 0.0), axis=1, keepdims=True)
    g_idx = jnp.min(jnp.where(is_g & (gl == gmax), lane, big), axis=1, keepdims=True)
    lo = N_GROUPS + g_idx * EXPERTS_PER_GROUP
    in_grp = (lane >= lo) & (lane < lo + EXPERTS_PER_GROUP)
    el = jnp.where(in_grp, logits, neg)
    l1 = jnp.max(el, axis=1, keepdims=True)
    i1 = jnp.min(jnp.where(in_grp & (el == l1), lane, big), axis=1, keepdims=True)
    el2 = jnp.where(lane == i1, neg, el)
    l2 = jnp.max(el2, axis=1, keepdims=True)
    i2 = jnp.min(jnp.where(in_grp & (lane != i1) & (el2 == l2), lane, big), axis=1, keepdims=True)
    p2 = jnp.exp(l2 - l1)
    w1 = g_val / (1.0 + p2)
    w2 = g_val * p2 / (1.0 + p2)
    return jnp.where(lane == i1, w1, jnp.where(lane == i2, w2, 0.0))


def _moe_dense(y, g, w_route, w_gate, w_up, w_down, *, precise=False, tm=1024, final_g=None):
    m, d = y.shape
    tm = min(tm, m)
    ne, _, ff = w_gate.shape
    xdt = F32 if precise else BF16

    def body(*refs):
        y_ref, g_ref, wr_ref, wg_ref, wu_ref, wd_ref = refs[:6]
        pos = 6
        fg_ref = None
        if final_g is not None:
            fg_ref = refs[6]
            pos = 7
        o_ref = refs[pos]
        pos += 1
        on_ref = None
        if final_g is not None:
            on_ref = refs[pos]
            pos += 1
        xn_scr, gate_scr, acc_scr = refs[pos:pos + 3]
        e = pl.program_id(1)

        @pl.when(e == 0)
        def _():
            xn = _rms(y_ref[...], g_ref[...])
            xn_scr[...] = xn.astype(xdt)
            logits = _dot_hi(xn, wr_ref[...]) if precise else _dot(xn.astype(BF16), wr_ref[...])
            gate_scr[...] = _route(logits)
            acc_scr[...] = jnp.zeros_like(acc_scr)

        xn = xn_scr[...]
        if precise:
            hg = _dot_hi(xn, wg_ref[0])
            hu = _dot_hi(xn, wu_ref[0])
            out = _dot_hi(_silu(hg) * hu, wd_ref[0])
        else:
            hg = _dot(xn, wg_ref[0])
            hu = _dot(xn, wu_ref[0])
            out = _dot((_silu(hg) * hu).astype(BF16), wd_ref[0])
        lane = _iota((tm, LANES), 1)
        ge = jnp.sum(jnp.where(lane == N_GROUPS + e, gate_scr[...], 0.0), axis=1, keepdims=True)
        acc_scr[...] += ge * out

        @pl.when(e == ne - 1)
        def _():
            res = y_ref[...] + acc_scr[...]
            o_ref[...] = res
            if on_ref is not None:
                on_ref[...] = _rms(res, fg_ref[...])

    tok = pl.BlockSpec((tm, d), lambda i, e: (i, 0))
    in_specs = [tok, pl.BlockSpec((1, d), lambda i, e: (0, 0)), pl.BlockSpec(w_route.shape, lambda i, e: (0, 0)),
                pl.BlockSpec((1, d, ff), lambda i, e: (e, 0, 0)), pl.BlockSpec((1, d, ff), lambda i, e: (e, 0, 0)),
                pl.BlockSpec((1, ff, d), lambda i, e: (e, 0, 0))]
    args = [y, g, w_route, w_gate, w_up, w_down]
    out_shape = [jax.ShapeDtypeStruct((m, d), F32)]
    out_specs = [tok]
    if final_g is not None:
        in_specs.append(pl.BlockSpec((1, d), lambda i, e: (0, 0)))
        args.append(final_g)
        out_shape.append(jax.ShapeDtypeStruct((m, d), F32))
        out_specs.append(tok)
    res = pl.pallas_call(
        body, grid=(m // tm, ne), in_specs=in_specs, out_specs=out_specs, out_shape=out_shape,
        scratch_shapes=[pltpu.VMEM((tm, d), xdt), pltpu.VMEM((tm, LANES), F32), pltpu.VMEM((tm, d), F32)],
        compiler_params=_cparams("parallel", "arbitrary"), name="moe_dense")(*args)
    return res[0] if final_g is None else res


def _heads3(x):
    return x.reshape(N_HEADS, HEAD_DIM, x.shape[-1])


def _retention_sample(q_col, k_col, v, gate, state, g_ret, inv_col, pos):
    b, w, _ = q_col.shape
    scale = HEAD_DIM ** -0.5
    half = HEAD_DIM // 2

    def body(q_ref, k_ref, v_ref, gate_ref, s_ref, gr_ref, inv_ref, o_ref, so_ref):
        row = _iota((w, 1), 0)
        first_half = (row % HEAD_DIM) < half
        ang = float(pos) * inv_ref[...]
        cos, sin = jnp.cos(ang), jnp.sin(ang)

        def rot(x):
            return x * cos + jnp.where(first_half, -pltpu.roll(x, w - half, 0), pltpu.roll(x, half, 0)) * sin

        q = rot(q_ref[0])
        k = rot(k_ref[0]) * scale
        gamma = 1.0 - jnp.exp2(-5.0 - (row // HEAD_DIM).astype(F32))
        s = s_ref[0]
        vv = v_ref[0]
        qk = jnp.sum(_heads3(q * k), axis=1)
        qs = jnp.sum(_heads3(q * s * gamma), axis=1)
        o = qk * vv + qs
        v_rows = jnp.broadcast_to(vv[:, None, :], (N_HEADS, HEAD_DIM, HEAD_DIM)).reshape(w, HEAD_DIM)
        so_ref[0] = s * gamma + k * v_rows
        mu = jnp.mean(o, axis=-1, keepdims=True)
        dlt = o - mu
        var = jnp.mean(dlt * dlt, axis=-1, keepdims=True)
        o_ref[0] = dlt * lax.rsqrt(var + EPS) * gr_ref[...] * _silu(gate_ref[0])

    col = pl.BlockSpec((1, w, 1), lambda i: (i, 0, 0))
    hd = pl.BlockSpec((1, N_HEADS, HEAD_DIM), lambda i: (i, 0, 0))
    st = pl.BlockSpec((1, w, HEAD_DIM), lambda i: (i, 0, 0))
    return pl.pallas_call(
        body, grid=(b,),
        in_specs=[col, col, hd, hd, st, pl.BlockSpec((N_HEADS, HEAD_DIM), lambda i: (0, 0)),
                  pl.BlockSpec((w, 1), lambda i: (0, 0))],
        out_specs=[hd, st],
        out_shape=[jax.ShapeDtypeStruct((b, N_HEADS, HEAD_DIM), F32), jax.ShapeDtypeStruct((b, w, HEAD_DIM), F32)],
        compiler_params=_cparams("parallel"), name="retention_sample")(q_col, k_col, v, gate, state, g_ret, inv_col)


def _gdn_sample(xqk_col, bufqk_col, cwqk_col, xv, bufv, cwv, dz, small_col, bias_col, alog_col, state, g_gdn):
    b = xqk_col.shape[0]
    w = N_HEADS * HEAD_DIM
    scale = HEAD_DIM ** -0.5
    r = small_col.shape[1]

    def body(x_ref, bq_ref, cq_ref, xv_ref, bv_ref, cv_ref, dz_ref, sm_ref, bc_ref, ac_ref, s_ref, gg_ref,
             o_ref, so_ref):
        yqk = x_ref[0] * cq_ref[CONV_W - 1]
        yv = xv_ref[0] * cv_ref[CONV_W - 1]
        for j in range(CONV_W - 1):
            yqk = yqk + bq_ref[0, j] * cq_ref[j]
            yv = yv + bv_ref[0, j] * cv_ref[j]
        yqk = _silu(yqk)
        vv = _silu(yv)

        def l2n(a):
            a3 = _heads3(a)
            ss = jnp.sum(a3 * a3, axis=1, keepdims=True)
            return a3 * lax.rsqrt(ss + EPS)

        q3 = l2n(yqk[:w]) * scale
        k3 = l2n(yqk[w:])
        post = _gate_math(sm_ref[0], bc_ref[...], ac_ref[...], _iota((r, 1), 0))
        g = post[N_HEADS:2 * N_HEADS]
        beta = post[2 * N_HEADS:3 * N_HEADS]
        a = jnp.exp(g)
        s3 = _heads3(s_ref[0])
        ks = jnp.sum(k3 * s3, axis=1)
        v_new = beta * vv - (beta * a) * ks
        o = a * jnp.sum(q3 * s3, axis=1) + jnp.sum(q3 * k3, axis=1) * v_new
        s_new = s3 * a[:, :, None] + k3 * v_new[:, None, :]
        so_ref[0] = s_new.reshape(w, HEAD_DIM)
        y = o * lax.rsqrt(jnp.mean(o * o, axis=-1, keepdims=True) + EPS) * gg_ref[...]
        o_ref[0] = y * _silu(dz_ref[0])

    hd = pl.BlockSpec((1, N_HEADS, HEAD_DIM), lambda i: (i, 0, 0))
    st = pl.BlockSpec((1, w, HEAD_DIM), lambda i: (i, 0, 0))
    return pl.pallas_call(
        body, grid=(b,),
        in_specs=[pl.BlockSpec((1, 2 * w, 1), lambda i: (i, 0, 0)),
                  pl.BlockSpec((1, CONV_W - 1, 2 * w, 1), lambda i: (i, 0, 0, 0)),
                  pl.BlockSpec((CONV_W, 2 * w, 1), lambda i: (0, 0, 0)),
                  hd, pl.BlockSpec((1, CONV_W - 1, N_HEADS, HEAD_DIM), lambda i: (i, 0, 0, 0)),
                  pl.BlockSpec((CONV_W, N_HEADS, HEAD_DIM), lambda i: (0, 0, 0)),
                  hd, pl.BlockSpec((1, r, 1), lambda i: (i, 0, 0)),
                  pl.BlockSpec((r, 1), lambda i: (0, 0)), pl.BlockSpec((r, 1), lambda i: (0, 0)),
                  st, pl.BlockSpec((1, HEAD_DIM), lambda i: (0, 0))],
        out_specs=[hd, st],
        out_shape=[jax.ShapeDtypeStruct((b, N_HEADS, HEAD_DIM), F32), jax.ShapeDtypeStruct((b, w, HEAD_DIM), F32)],
        compiler_params=_cparams("parallel"), name="gdn_sample")(
            xqk_col, bufqk_col, cwqk_col, xv, bufv, cwv, dz, small_col, bias_col, alog_col, state, g_gdn)


PAGES_PER_STEP = 16


def _paged_sweep(mode, q_col, cache_kt, cache_vt, page_table, page_base, *, k_new=None, v_new=None,
                 cache_lft=None, logf_new=None):
    fox = mode == "fox"
    b, nh, hd, _ = q_col.shape
    n_pages = page_table.shape[1]
    page = cache_kt.shape[-1]
    pps = math.gcd(n_pages, PAGES_PER_STEP)
    ng = n_pages // pps
    qscale = hd ** -0.5 * LOG2E
    n_split = 3 if fox else 2

    def body(*refs):
        q_ref = refs[1]
        pos = 2
        k_refs = refs[pos:pos + pps]
        pos += pps
        v_refs = refs[pos:pos + pps]
        pos += pps
        if fox:
            lf_refs = refs[pos:pos + pps]
            pos += pps
            kn_ref, vn_ref, lfn_ref = refs[pos:pos + 3]
            pos += 3
        o_ref = refs[pos]
        pos += 1
        acc_scr, carry_scr = refs[pos:pos + 2]
        pos += 2
        if fox:
            m_scr, l_scr = refs[pos:pos + 2]
        j = pl.program_id(1)
        q3 = q_ref[0] * qscale
        li = _iota((page, page), 0)
        lj = _iota((page, page), 1)
        later = jnp.where((li > lj) if fox else (li >= lj), 1.0, 0.0).astype(BF16)
        later = jnp.concatenate([later] * n_split, axis=0)

        @pl.when(j == 0)
        def _():
            if fox:
                m_scr[...] = jnp.sum(q3 * kn_ref[0], axis=1)
                l_scr[...] = jnp.ones_like(l_scr)
                acc_scr[...] = jnp.where(_iota((nh, hd, page), 2) == 0, vn_ref[0], 0.0)
                carry_scr[...] = lfn_ref[0] * LOG2E
            else:
                acc_scr[...] = jnp.zeros_like(acc_scr)
                carry_scr[...] = jnp.zeros_like(carry_scr)

        pages = range(pps)
        zs = [jnp.sum(k_refs[p][0] * q3, axis=1) for p in pages]
        carry = carry_scr[...]
        if not fox:
            sps = [_softplus2(z) for z in zs]
            laters = [_dot(jnp.concatenate(_split2(sp), axis=1), later) for sp in sps]
            wts = []
            for p in pages:
                wts.append(jnp.exp2(zs[p] - laters[p] - carry))
                carry = carry + jnp.sum(sps[p], axis=1, keepdims=True)
            acc = acc_scr[...]
        else:
            lfs = [lf_refs[p][0] * LOG2E for p in pages]
            laters = [_dot(jnp.concatenate(_split3(lf), axis=1), later) for lf in lfs]
            logits = []
            for p in pages:
                logits.append(zs[p] + (laters[p] + carry))
                carry = carry + jnp.sum(lfs[p], axis=1, keepdims=True)
            m_old = m_scr[...]
            m_new = m_old
            for s in logits:
                m_new = jnp.maximum(m_new, jnp.max(s, axis=1, keepdims=True))
            alpha = jnp.exp2(m_old - m_new)
            wts = [jnp.exp2(s - m_new) for s in logits]
            l_new = alpha * l_scr[...]
            for wt in wts:
                l_new = l_new + jnp.sum(wt, axis=1, keepdims=True)
            l_scr[...] = l_new
            m_scr[...] = m_new
            acc = acc_scr[...] * alpha[:, :, None]
        carry_scr[...] = carry
        for p in pages:
            acc = acc + wts[p][:, None, :] * v_refs[p][0]
        acc_scr[...] = acc

        @pl.when(j == ng - 1)
        def _():
            res = jnp.sum(acc_scr[...], axis=2, keepdims=True)
            o_ref[0] = res / l_scr[...][:, :, None] if fox else res

    def page_map(p, ndim):
        def index_map(i, j, pt):
            return (page_base + pt[i, (ng - 1 - j) * pps + (pps - 1 - p)],) + (0,) * (ndim - 1)
        return index_map

    col_spec = pl.BlockSpec((1, nh, hd, 1), lambda i, j, pt: (i, 0, 0, 0))
    in_specs = [col_spec]
    in_specs += [pl.BlockSpec((1, nh, hd, page), page_map(p, 4)) for p in range(pps)]
    in_specs += [pl.BlockSpec((1, nh, hd, page), page_map(p, 4)) for p in range(pps)]
    args = [q_col] + [cache_kt] * pps + [cache_vt] * pps
    scratch = [pltpu.VMEM((nh, hd, page), F32), pltpu.VMEM((nh, 1), F32)]
    if fox:
        in_specs += [pl.BlockSpec((1, nh, page), page_map(p, 3)) for p in range(pps)]
        in_specs += [col_spec, col_spec, pl.BlockSpec((1, nh, 1), lambda i, j, pt: (i, 0, 0))]
        args += [cache_lft] * pps + [k_new, v_new, logf_new]
        scratch += [pltpu.VMEM((nh, 1), F32), pltpu.VMEM((nh, 1), F32)]
    return pl.pallas_call(
        body,
        grid_spec=pltpu.PrefetchScalarGridSpec(
            num_scalar_prefetch=1, grid=(b, ng), in_specs=in_specs, out_specs=col_spec, scratch_shapes=scratch),
        out_shape=jax.ShapeDtypeStruct((b, nh, hd, 1), F32),
        compiler_params=_cparams("parallel", "arbitrary"), name="paged_sweep_" + mode)(page_table, *args)


def kernel(x_prompt, x_sample, state_ret, cache_sb_k, cache_sb_v, cache_fox_k, cache_fox_v, cache_fox_logf, state_gdn, state_conv, cache_mem_k, cache_mem_v, page_table, mem_prompt, g_mix, ev_w_in, ev_g_ret, ev_w_out, od_w_in, od_b_forget, od_conv_w, od_a_log, od_dt_bias, od_g_gdn, od_w_out, g_xattn, g_mem, w_mq, w_mk, w_mv, w_mo, g_ffn, w_route_group, w_route_expert, w_exp_gate, w_exp_up, w_exp_down, g_final):
    bp, t, d = x_prompt.shape
    bs, ts, _ = x_sample.shape
    assert bp == 1 and ts == 1 and t >= CONV_W - 1
    depth = g_mix.shape[0]
    n_pages = page_table.shape[1]
    n_phys, page = cache_sb_k.shape[1], cache_sb_k.shape[2]
    past_len = n_pages * page
    w = MIX_HALF
    nh, hd = N_HEADS, HEAD_DIM
    attn_scale = hd ** -0.5 * LOG2E
    f32o = [(F32, 1.0)]
    both = [(F32, 1.0), (BF16, 1.0)]
    row2 = lambda v: v.reshape(1, -1)

    def token_minor(cache):
        return jnp.transpose(cache, (0, 1, 3, 4, 2)).reshape(-1, nh, hd, page)

    half = hd // 2
    inv = ROPE_BASE ** (-jnp.arange(half, dtype=F32) / half)
    inv_lane = jnp.tile(inv, LANES // half)[None]
    inv_col = jnp.tile(inv, w // half)[:, None]

    yp = x_prompt[0]
    ys = x_sample[:, 0]
    y_prompt = y_sample = None
    ret_p, sbk_p, sbv_p, foxk_p, foxv_p, foxl_p, gdn_p, conv_p, memk_p, memv_p = ([] for _ in range(10))
    ret_s, sbk_s, sbv_s, foxk_s, foxv_s, foxl_s, gdn_s, conv_s = ([] for _ in range(8))

    for layer in range(depth):
        i = layer // 2
        g_in = row2(g_mix[layer])
        if layer % 2 == 0:
            w_in = ev_w_in[i]
            w_out = ev_w_out[i]
            outs = [(0, w, f32o), (w, w, f32o), (2 * w, w, f32o), (3 * w, w, f32o),
                    (4 * w, w, [(BF16, attn_scale)]), (5 * w, w, both), (6 * w, w, both)]
            rq, rk, rv, rg, sqb, sk, skb, sv, svb = _norm_proj(yp, g_in, w_in.astype(BF16), outs)
            o_ret, s_ret = _retention_prompt(rq, rk, rv, rg, row2(ev_g_ret[i]), inv_lane)
            o_sb = _sb_prompt(sqb, skb, svb)
            w_out_b = w_out.astype(BF16)
            yp = _resid_proj(yp, [o_ret, o_sb], [w_out_b[:w], w_out_b[w:]])
            ret_p.append(s_ret[None])
            sbk_p.append(sk.reshape(1, t, nh, hd))
            sbv_p.append(sv.reshape(1, t, nh, hd))
            outs = [(c * w, w, f32o) for c in range(7)]
            rq, rk, rv, rg, sq, sk, sv = _norm_proj(ys, g_in, w_in, outs, precise=True)
            o_ret, s_new = _retention_sample(
                rq.reshape(bs, w, 1), rk.reshape(bs, w, 1), rv.reshape(bs, nh, hd), rg.reshape(bs, nh, hd),
                state_ret[i].reshape(bs, w, hd), ev_g_ret[i].reshape(nh, hd), inv_col, past_len)
            o_sb = _paged_sweep("sb", sq.reshape(bs, nh, hd, 1), token_minor(cache_sb_k), token_minor(cache_sb_v),
                                page_table, i * n_phys)
            ys = _resid_proj(ys, [o_ret.reshape(bs, w), o_sb.reshape(bs, w)], [w_out[:w], w_out[w:]], precise=True)
            ret_s.append(s_new.reshape(bs, nh, hd, hd))
            sbk_s.append(sk.reshape(bs, 1, nh, hd))
            sbv_s.append(sv.reshape(bs, 1, nh, hd))
        else:
            w_full = od_w_in[i]
            w_out = od_w_out[i]
            c_fl = 3 * w
            c_qkv = c_fl + nh
            c_dz = c_qkv + 3 * w
            c_da = c_dz + w
            n_small = 3 * nh
            small_w = jnp.concatenate([w_full[:, c_fl:c_qkv], w_full[:, c_da:c_da + 2 * nh],
                                       jnp.zeros((d, LANES - n_small), F32)], axis=1)
            w_in = jnp.concatenate([w_full[:, :c_fl], w_full[:, c_qkv:c_da], small_w], axis=1)
            o_dqkv, o_dz, o_small = 3 * w, 6 * w, 7 * w
            zeros8 = jnp.zeros((nh,), F32)
            bias = jnp.concatenate([od_b_forget[i], od_dt_bias[i], zeros8])
            alog = jnp.concatenate([zeros8, od_a_log[i], zeros8])
            n_rows = 4 * nh
            bias_lane = jnp.pad(bias, (0, LANES - n_small))[None]
            alog_lane = jnp.pad(alog, (0, LANES - n_small))[None]
            bias_col = jnp.pad(bias, (0, n_rows - n_small))[:, None]
            alog_col = jnp.pad(alog, (0, n_rows - n_small))[:, None]
            outs = [(0, w, [(BF16, attn_scale)]), (w, w, both), (2 * w, w, both),
                    (o_dqkv, 3 * w, f32o), (o_dz, w, f32o), (o_small, LANES, f32o)]
            w_in_b = w_in.astype(BF16)
            small_wt = small_w[:, :n_rows].T.astype(BF16)
            fqb, fk, fkb, fv, fvb, dqkv, dz, small_col, small_row = _norm_proj(yp, g_in, w_in_b, outs, wt=small_wt)
            post_col, post_row = _gates(small_col, small_row, bias_lane, alog_lane, bias_col, alog_col)
            n_chunks = t // LANES
            cum_row = _cum_forget(post_row[:nh].reshape(nh * n_chunks, LANES), n_chunks).reshape(nh, t)
            o_fox = _fox_prompt(fqb, fkb, fvb, cum_row)
            gq, gk, gkt, gv = _gdn_prep(dqkv, od_conv_w[i])
            wy = _gdn_wy(gq, gk, gkt, gv, post_col, post_row)
            o_gdn, s_gdn = _gdn_scan(*wy, dz, row2(od_g_gdn[i]))
            w_out_b = w_out.astype(BF16)
            yp = _resid_proj(yp, [o_fox, o_gdn], [w_out_b[:w], w_out_b[w:]])
            foxk_p.append(fk.reshape(1, t, nh, hd))
            foxv_p.append(fv.reshape(1, t, nh, hd))
            foxl_p.append(post_col[:, :nh].reshape(1, t, nh))
            gdn_p.append(s_gdn[None])
            conv_p.append(dqkv[t - (CONV_W - 1):][None])
            outs = [(0, w, f32o), (w, w, f32o), (2 * w, w, f32o), (o_dqkv, 3 * w, f32o), (o_dz, w, f32o),
                    (o_small, LANES, f32o)]
            fq, fk, fv, dqkv, dz, small = _norm_proj(ys, g_in, w_in, outs, precise=True)
            post_s, _ = _gates(small, small[:, :n_rows].T, bias_lane, alog_lane, bias_col, alog_col)
            o_fox = _paged_sweep(
                "fox", fq.reshape(bs, nh, hd, 1), token_minor(cache_fox_k), token_minor(cache_fox_v),
                page_table, i * n_phys, k_new=fk.reshape(bs, nh, hd, 1), v_new=fv.reshape(bs, nh, hd, 1),
                cache_lft=jnp.swapaxes(cache_fox_logf, 2, 3).reshape(-1, nh, page),
                logf_new=post_s[:, :nh].reshape(bs, nh, 1))
            buf = state_conv[i]
            cw = od_conv_w[i]
            o_gdn, s_new = _gdn_sample(
                dqkv[:, :2 * w].reshape(bs, 2 * w, 1), buf[:, :, :2 * w].reshape(bs, CONV_W - 1, 2 * w, 1),
                cw[:, :2 * w].reshape(CONV_W, 2 * w, 1), dqkv[:, 2 * w:].reshape(bs, nh, hd),
                buf[:, :, 2 * w:].reshape(bs, CONV_W - 1, nh, hd), cw[:, 2 * w:].reshape(CONV_W, nh, hd),
                dz.reshape(bs, nh, hd), small[:, :n_rows].reshape(bs, n_rows, 1), bias_col, alog_col,
                state_gdn[i].reshape(bs, w, hd), row2(od_g_gdn[i]))
            ys = _resid_proj(ys, [o_fox.reshape(bs, w), o_gdn.reshape(bs, w)], [w_out[:w], w_out[w:]], precise=True)
            foxk_s.append(fk.reshape(bs, 1, nh, hd))
            foxv_s.append(fv.reshape(bs, 1, nh, hd))
            foxl_s.append(post_s[:, :nh].reshape(bs, 1, nh))
            gdn_s.append(s_new.reshape(bs, nh, hd, hd))
            conv_s.append(jnp.concatenate([buf[:, 1:], dqkv[:, None]], axis=1))

        n_mem = mem_prompt.shape[1]
        mw = MEM_HEADS * MEM_HEAD_DIM
        w_kv = jnp.concatenate([w_mk[layer], w_mv[layer]], axis=1).astype(BF16)
        mk, mkb, mv, mvb = _norm_proj(mem_prompt[0], row2(g_mem[layer]), w_kv, [(0, mw, both), (mw, mw, both)])
        memk_p.append(mk.reshape(1, n_mem, MEM_HEADS, MEM_HEAD_DIM))
        memv_p.append(mv.reshape(1, n_mem, MEM_HEADS, MEM_HEAD_DIM))
        g_x = row2(g_xattn[layer])
        yp = _mem_attn_prompt(yp, g_x, w_mq[layer].astype(BF16), mkb, mvb, w_mo[layer].astype(BF16))
        (q_s,) = _norm_proj(ys, g_x, w_mq[layer], [(0, mw, f32o)], precise=True)
        o_mem = _mem_attn_sample(q_s.reshape(bs, 1, mw), cache_mem_k[layer].reshape(bs, n_mem, mw),
                                 cache_mem_v[layer].reshape(bs, n_mem, mw))
        ys = _resid_proj(ys, [o_mem.reshape(bs, mw)], [w_mo[layer]], precise=True)

        w_route = jnp.concatenate([w_route_group[layer], w_route_expert[layer],
                                   jnp.zeros((d, LANES - N_GROUPS - N_EXPERTS), F32)], axis=1)
        g_f = row2(g_ffn[layer])
        fin = row2(g_final) if layer == depth - 1 else None
        res_p = _moe_dense(yp, g_f, w_route.astype(BF16), w_exp_gate[layer].astype(BF16),
                           w_exp_up[layer].astype(BF16), w_exp_down[layer].astype(BF16), final_g=fin)
        res_s = _moe_dense(ys, g_f, w_route, w_exp_gate[layer], w_exp_up[layer], w_exp_down[layer],
                           precise=True, final_g=fin)
        if fin is None:
            yp, ys = res_p, res_s
        else:
            (yp, y_prompt), (ys, y_sample) = res_p, res_s

    return (y_prompt[None], y_sample[:, None],
            jnp.stack(ret_p), jnp.stack(sbk_p), jnp.stack(sbv_p),
            jnp.stack(foxk_p), jnp.stack(foxv_p), jnp.stack(foxl_p), jnp.stack(gdn_p), jnp.stack(conv_p),
            jnp.stack(memk_p), jnp.stack(memv_p),
            jnp.stack(ret_s), jnp.stack(sbk_s), jnp.stack(sbv_s),
            jnp.stack(foxk_s), jnp.stack(foxv_s), jnp.stack(foxl_s), jnp.stack(gdn_s), jnp.stack(conv_s))
```
